```python
import math
import jax, jax.numpy as jnp
from jax import lax
import numpy as np

D_MODEL = 1024
BATCH = 16
SEQ = 2048
DEPTH = 1
DEC_BATCH = 128
DEC_SEQ = 1
PAST_LEN = 8192
PAGE_SIZE = 128

N_HEADS = 8
HEAD_DIM = 64
ATTN_DIM = N_HEADS * HEAD_DIM
IDX_HEADS = 8
IDX_DIM = 64
IDX_TOPK_MAX = 256
IDX_Q_BLOCK = 32
REL_BUCKETS = 32
REL_MAX_DIST = 128
CONV_DIM = 512
CONV_WIDTH = 31
CONV_STATE = CONV_WIDTH - 1
PEER_HEADS = 8
PEER_KEYS = 128
PEER_EXPERTS = PEER_KEYS * PEER_KEYS
PEER_KEY_DIM = 128
PEER_HALF = PEER_KEY_DIM // 2
PEER_TOPK = 16
PEER_TOK_BLOCK = 256
EPS = 1e-6
IN_SPLITS = (2 * CONV_DIM, ATTN_DIM, ATTN_DIM, ATTN_DIM, IDX_HEADS * IDX_DIM, IDX_DIM, IDX_HEADS, D_MODEL, D_MODEL)
IN_COLS = 2 * CONV_DIM + 4 * ATTN_DIM + IDX_DIM + IDX_HEADS + 2 * D_MODEL

kernel_name = 'hybrid_conv_dsa_peer_decoder_step'


def rmsnorm(x, g):
    x32 = x.astype(jnp.float32)
    y = x32 * lax.rsqrt(jnp.mean(x32 * x32, axis=-1, keepdims=True) + EPS)
    return y.astype(x.dtype) * g


def layernorm(x, g, b):
    x32 = x.astype(jnp.float32)
    mu = jnp.mean(x32, axis=-1, keepdims=True)
    xc = x32 - mu
    y = xc * lax.rsqrt(jnp.mean(xc * xc, axis=-1, keepdims=True) + EPS)
    return y.astype(x.dtype) * g + b


def adaln(c, w_ada, b_ada):
    m = jax.nn.silu(c) @ w_ada + b_ada
    return jnp.split(m[:, None, :], 6, axis=-1)


def in_projection(x, mod, norm1_g, w_in, q_norm_g, k_norm_g):
    B, T, _ = x.shape
    h = rmsnorm(x, norm1_g) * (1 + mod[1]) + mod[0]
    pts = [int(p) for p in np.cumsum(IN_SPLITS)[:-1]]
    glu_in, q, k, v, qi, ki, wi, g_conv, g_attn = jnp.split(h @ w_in, pts, axis=-1)
    glu = glu_in[..., :CONV_DIM] * jax.nn.sigmoid(glu_in[..., CONV_DIM:])
    q = rmsnorm(q.reshape(B, T, N_HEADS, HEAD_DIM), q_norm_g)
    k = rmsnorm(k.reshape(B, T, N_HEADS, HEAD_DIM), k_norm_g)
    v = v.reshape(B, T, N_HEADS, HEAD_DIM)
    qi = qi.reshape(B, T, IDX_HEADS, IDX_DIM)
    wi = wi * (IDX_HEADS ** -0.5)
    return glu, q, k, v, qi, ki, wi, g_conv, g_attn


def conv_tail(glu_ext, w_dw, b_dw, ln_g, ln_b, w_conv_o):
    y = lax.conv_general_dilated(glu_ext, w_dw[:, None, :], (1,), 'VALID',
                                 dimension_numbers=('NWC', 'WIO', 'NWC'),
                                 feature_group_count=CONV_DIM) + b_dw
    y = jax.nn.silu(layernorm(y, ln_g, ln_b))
    return y @ w_conv_o


def rel_bucket(dist):
    max_exact = REL_BUCKETS // 2
    d = jnp.maximum(dist, 0)
    df = jnp.maximum(d, 1).astype(jnp.float32)
    large = max_exact + (jnp.log(df / max_exact) / math.log(REL_MAX_DIST / max_exact)
                         * (REL_BUCKETS - max_exact)).astype(jnp.int32)
    return jnp.where(d < max_exact, d, jnp.minimum(large, REL_BUCKETS - 1))


def index_select(qi, wi, ki, t_pos, topk):
    s = jnp.einsum('bqhd,bld->bqhl', qi, ki).astype(jnp.float32) * (IDX_DIM ** -0.5)
    score = jnp.einsum('bqhl,bqh->bql', jax.nn.relu(s), wi.astype(jnp.float32))
    key_pos = jnp.arange(ki.shape[1], dtype=jnp.int32)
    causal = key_pos[None, :] <= t_pos[:, None]
    score = jnp.where(causal[None], score, -jnp.inf)
    _, idx = lax.top_k(score, topk)
    valid = idx <= t_pos[None, :, None]
    return idx, valid


def sparse_attend(q, k_sel, v_sel, t_pos, idx, valid, rel_bias):
    logits = jnp.einsum('bqhd,bqkhd->bqkh', q, k_sel).astype(jnp.float32) * (HEAD_DIM ** -0.5)
    logits = logits + rel_bias[rel_bucket(t_pos[None, :, None] - idx)].astype(jnp.float32)
    logits = jnp.where(valid[..., None], logits, -jnp.inf)
    p = jax.nn.softmax(logits, axis=2).astype(v_sel.dtype)
    return jnp.einsum('bqkh,bqkhd->bqhd', p, v_sel)


def prompt_attention(q, k, v, qi, ki, wi, rel_bias):
    B, S = q.shape[:2]
    topk = min(IDX_TOPK_MAX, S // 4)
    nb = S // IDX_Q_BLOCK

    def blocks(a):
        return jnp.moveaxis(a.reshape((B, nb, IDX_Q_BLOCK) + a.shape[2:]), 1, 0)

    t_blocks = jnp.arange(S, dtype=jnp.int32).reshape(nb, IDX_Q_BLOCK)
    take = jax.vmap(lambda a, i: a[i])

    def one_block(args):
        qb, qib, wib, tb = args
        idx, valid = index_select(qib, wib, ki, tb, topk)
        return sparse_attend(qb, take(k, idx), take(v, idx), tb, idx, valid, rel_bias)

    out = lax.map(one_block, (blocks(q), blocks(qi), blocks(wi), t_blocks))
    return jnp.moveaxis(out, 0, 1).reshape(B, S, N_HEADS, HEAD_DIM)


def sample_attention(q, k, v, qi, ki, wi, cache_k, cache_v, cache_idx_k, page_table, rel_bias):
    DB, DS = q.shape[:2]
    n_pages = page_table.shape[1]
    past = n_pages * PAGE_SIZE
    topk = min(IDX_TOPK_MAX, (past + DS) // 4)
    ki_past = cache_idx_k[page_table].reshape(DB, past, IDX_DIM)
    ki_all = jnp.concatenate([ki_past, ki], axis=1)
    t_pos = past + jnp.arange(DS, dtype=jnp.int32)
    idx, valid = index_select(qi, wi, ki_all, t_pos, topk)
    in_past = idx < past
    page = jnp.minimum(idx // PAGE_SIZE, n_pages - 1)
    off = idx % PAGE_SIZE
    phys = jax.vmap(lambda pt, p: pt[p])(page_table, page)
    take = jax.vmap(lambda a, i: a[i])
    j = jnp.clip(idx - past, 0, DS - 1)
    sel = in_past[..., None, None]
    k_sel = jnp.where(sel, cache_k[phys, off], take(k, j))
    v_sel = jnp.where(sel, cache_v[phys, off], take(v, j))
    return sparse_attend(q, k_sel, v_sel, t_pos, idx, valid, rel_bias)


def peer_block(h, w_peer_q, peer_keys, peer_u, peer_v):
    T = h.shape[0]
    q = (h @ w_peer_q).reshape(T, PEER_HEADS, 2, PEER_HALF)
    s = jnp.einsum('thpd,hpnd->thpn', q, peer_keys).astype(jnp.float32)
    sv, si = lax.top_k(s, PEER_TOPK)
    cand = sv[:, :, 0, :, None] + sv[:, :, 1, None, :]
    cid = si[:, :, 0, :, None] * PEER_KEYS + si[:, :, 1, None, :]
    top_s, top_p = lax.top_k(cand.reshape(T, PEER_HEADS, PEER_TOPK * PEER_TOPK), PEER_TOPK)
    eid = jnp.take_along_axis(cid.reshape(T, PEER_HEADS, PEER_TOPK * PEER_TOPK), top_p, axis=-1)
    g = jax.nn.softmax(top_s, axis=-1).astype(h.dtype)
    a = jax.nn.gelu(jnp.einsum('td,thkd->thk', h, peer_u[eid]), approximate=False)
    return jnp.einsum('thk,thkd->td', g * a, peer_v[eid])


def peer(h, w_peer_q, peer_keys, peer_u, peer_v):
    n = h.shape[0]
    nb = -(-n // PEER_TOK_BLOCK)
    hp = jnp.pad(h, ((0, nb * PEER_TOK_BLOCK - n), (0, 0))).reshape(nb, PEER_TOK_BLOCK, D_MODEL)
    out = lax.map(lambda hb: peer_block(hb, w_peer_q, peer_keys, peer_u, peer_v), hp)
    return out.reshape(nb * PEER_TOK_BLOCK, D_MODEL)[:n]


def merge_and_channel_mix(x, mod, conv_out, attn_heads, g_conv, g_attn, w_attn_o, w_out,
                          norm2_g, w_peer_q, peer_keys, peer_u, peer_v):
    B, T, _ = x.shape
    attn_out = attn_heads.reshape(B, T, ATTN_DIM) @ w_attn_o
    merged = jax.nn.sigmoid(g_conv) * conv_out + jax.nn.sigmoid(g_attn) * attn_out
    x = x + mod[2] * (merged @ w_out)
    h2 = rmsnorm(x, norm2_g) * (1 + mod[4]) + mod[3]
    y = peer(h2.reshape(B * T, D_MODEL), w_peer_q, peer_keys, peer_u, peer_v).reshape(B, T, D_MODEL)
    return x + mod[5] * y


def setup_inputs(seed: int = 0) -> dict:
    key = jax.random.key(seed)
    ks = jax.random.split(key, 32)
    f32 = jnp.float32

    def nrm(k, shape, scale):
        return jax.random.normal(k, shape, f32) * scale

    n_pages = PAST_LEN // PAGE_SIZE
    n_used = DEC_BATCH * n_pages
    n_phys = n_used + max(1, n_used // 4)
    perm = jax.random.permutation(ks[0], n_phys)
    page_table = perm[:n_used].reshape(DEC_BATCH, n_pages).astype(jnp.int32)
    return {
        'x_prompt': nrm(ks[1], (BATCH, SEQ, D_MODEL), 1.0),
        'x_sample': nrm(ks[2], (DEC_BATCH, DEC_SEQ, D_MODEL), 1.0),
        'cache_k': nrm(ks[3], (DEPTH, n_phys, PAGE_SIZE, N_HEADS, HEAD_DIM), 1.0),
        'cache_v': nrm(ks[4], (DEPTH, n_phys, PAGE_SIZE, N_HEADS, HEAD_DIM), 1.0),
        'cache_idx_k': nrm(ks[5], (DEPTH, n_phys, PAGE_SIZE, IDX_DIM), 1.0),
        'state_conv': nrm(ks[6], (DEPTH, DEC_BATCH, CONV_STATE, CONV_DIM), 0.5),
        'page_table': page_table,
        'c_prompt': nrm(ks[7], (BATCH, D_MODEL), 1.0),
        'c_sample': nrm(ks[8], (DEC_BATCH, D_MODEL), 1.0),
        'w_ada': nrm(ks[9], (DEPTH, D_MODEL, 6 * D_MODEL), D_MODEL ** -0.5),
        'b_ada': nrm(ks[10], (DEPTH, 6 * D_MODEL), 0.02),
        'norm1_g': 1.0 + nrm(ks[11], (DEPTH, D_MODEL), 0.02),
        'w_in': nrm(ks[12], (DEPTH, D_MODEL, IN_COLS), D_MODEL ** -0.5),
        'q_norm_g': 1.0 + nrm(ks[13], (DEPTH, HEAD_DIM), 0.02),
        'k_norm_g': 1.0 + nrm(ks[14], (DEPTH, HEAD_DIM), 0.02),
        'rel_bias': nrm(ks[15], (REL_BUCKETS, N_HEADS), 0.5),
        'w_dw': nrm(ks[16], (DEPTH, CONV_WIDTH, CONV_DIM), CONV_WIDTH ** -0.5),
        'b_dw': nrm(ks[17], (DEPTH, CONV_DIM), 0.02),
        'conv_ln_g': 1.0 + nrm(ks[18], (DEPTH, CONV_DIM), 0.02),
        'conv_ln_b': nrm(ks[19], (DEPTH, CONV_DIM), 0.02),
        'w_conv_o': nrm(ks[20], (DEPTH, CONV_DIM, D_MODEL), CONV_DIM ** -0.5),
        'w_attn_o': nrm(ks[21], (DEPTH, ATTN_DIM, D_MODEL), ATTN_DIM ** -0.5),
        'w_out': nrm(ks[22], (DEPTH, D_MODEL, D_MODEL), D_MODEL ** -0.5),
        'norm2_g': 1.0 + nrm(ks[23], (DEPTH, D_MODEL), 0.02),
        'w_peer_q': nrm(ks[24], (DEPTH, D_MODEL, PEER_HEADS * PEER_KEY_DIM), D_MODEL ** -0.5),
        'peer_keys': nrm(ks[25], (DEPTH, PEER_HEADS, 2, PEER_KEYS, PEER_HALF), PEER_HALF ** -0.5),
        'peer_u': nrm(ks[26], (DEPTH, PEER_EXPERTS, D_MODEL), D_MODEL ** -0.5),
        'peer_v': nrm(ks[27], (DEPTH, PEER_EXPERTS, D_MODEL), PEER_HEADS ** -0.5),
    }


def reference(x_prompt, x_sample, cache_k, cache_v, cache_idx_k, state_conv, page_table, c_prompt, c_sample,
              w_ada, b_ada, norm1_g, w_in, q_norm_g, k_norm_g, rel_bias, w_dw, b_dw, conv_ln_g, conv_ln_b,
              w_conv_o, w_attn_o, w_out, norm2_g, w_peer_q, peer_keys, peer_u, peer_v):
    xp, xs = x_prompt, x_sample
    kp, vp, ip, cp, ksl, vsl, isl, csl = [], [], [], [], [], [], [], []
    for l in range(DEPTH):
        mod = adaln(c_prompt, w_ada[l], b_ada[l])
        glu, q, k, v, qi, ki, wi, gc, ga = in_projection(xp, mod, norm1_g[l], w_in[l], q_norm_g[l], k_norm_g[l])
        conv_out = conv_tail(jnp.pad(glu, ((0, 0), (CONV_STATE, 0), (0, 0))),
                             w_dw[l], b_dw[l], conv_ln_g[l], conv_ln_b[l], w_conv_o[l])
        attn = prompt_attention(q, k, v, qi, ki, wi, rel_bias)
        xp = merge_and_channel_mix(xp, mod, conv_out, attn, gc, ga, w_attn_o[l], w_out[l], norm2_g[l],
                                   w_peer_q[l], peer_keys[l], peer_u[l], peer_v[l])
        kp.append(k)
        vp.append(v)
        ip.append(ki)
        cp.append(glu[:, -CONV_STATE:])
        mod = adaln(c_sample, w_ada[l], b_ada[l])
        glu, q, k, v, qi, ki, wi, gc, ga = in_projection(xs, mod, norm1_g[l], w_in[l], q_norm_g[l], k_norm_g[l])
        glu_ext = jnp.concatenate([state_conv[l].astype(glu.dtype), glu], axis=1)
        conv_out = conv_tail(glu_ext, w_dw[l], b_dw[l], conv_ln_g[l], conv_ln_b[l], w_conv_o[l])
        attn = sample_attention(q, k, v, qi, ki, wi, cache_k[l], cache_v[l], cache_idx_k[l], page_table, rel_bias)
        xs = merge_and_channel_mix(xs, mod, conv_out, attn, gc, ga, w_attn_o[l], w_out[l], norm2_g[l],
                                   w_peer_q[l], peer_keys[l], peer_u[l], peer_v[l])
        ksl.append(k)
        vsl.append(v)
        isl.append(ki)
        csl.append(glu_ext[:, -CONV_STATE:])
    k_prompt = jnp.stack(kp)
    v_prompt = jnp.stack(vp)
    idxk_prompt = jnp.stack(ip)
    conv_prompt = jnp.stack(cp)
    k_sample = jnp.stack(ksl)
    v_sample = jnp.stack(vsl)
    idxk_sample = jnp.stack(isl)
    conv_sample = jnp.stack(csl)
    return (xp, xs, k_prompt, v_prompt, idxk_prompt, conv_prompt, k_sample, v_sample, idxk_sample, conv_sample)
```

```python
import functools
import math

import numpy as np
import jax
import jax.numpy as jnp
from jax import lax
from jax.experimental import pallas as pl
from jax.experimental.pallas import tpu as pltpu

F32 = jnp.float32
BF = jnp.bfloat16
I32 = jnp.int32

D_MODEL = 1024
N_HEADS = 8
HEAD_DIM = 64
ATTN_DIM = N_HEADS * HEAD_DIM
IDX_HEADS = 8
IDX_DIM = 64
IDX_TOPK_MAX = 256
REL_BUCKETS = 32
REL_MAX_DIST = 128
CONV_DIM = 512
CONV_WIDTH = 31
CONV_STATE = CONV_WIDTH - 1
PEER_HEADS = 8
PEER_KEYS = 128
PEER_HALF = 64
PEER_TOPK = 16
PAGE_SIZE = 128
EPS = 1e-6

NEG = -1e30
INT_MIN = -(2 ** 31)
INT_MAX = 2 ** 31 - 1
LANES = 128
SUBLANES = 8
VMEM_LIMIT = 52 * 1024 * 1024


def _params(n_axes):
    return pltpu.CompilerParams(dimension_semantics=("arbitrary",) * n_axes,
                                vmem_limit_bytes=VMEM_LIMIT)


def _full(shape):
    zeros = (0,) * len(shape)
    return pl.BlockSpec(shape, lambda *_: zeros)


def _mod(mod_ref, k, per_row):
    return mod_ref[k] if per_row else mod_ref[0, k:k + 1, :]


def _mod_spec(per_row, tm, rows_per_seq):
    if per_row:
        return pl.BlockSpec((6, tm, D_MODEL), lambda i: (0, i, 0))
    tiles_per_seq = rows_per_seq // tm
    return pl.BlockSpec((1, 6, D_MODEL), lambda i: (i // tiles_per_seq, 0, 0))


def _dot(a, b):
    return jnp.dot(a, b, preferred_element_type=F32)


def _dot_nt(a, b):
    return lax.dot_general(a, b, (((1,), (1,)), ((), ())), preferred_element_type=F32)


def _monotone_key(x):
    bits = lax.bitcast_convert_type(x, I32)
    return bits ^ ((bits >> 31) & INT_MAX)


def _ada_kernel(c_ref, w_ref, b_ref, o_ref):
    c = c_ref[...]
    a = (c * jax.nn.sigmoid(c)).astype(BF)
    o_ref[...] = _dot(a, w_ref[...]) + b_ref[...]


def _ada(c, w_bf, b):
    rows = c.shape[0]
    n = w_bf.shape[1]
    nb = 1536
    return pl.pallas_call(
        _ada_kernel,
        grid=(n // nb,),
        in_specs=[_full((rows, D_MODEL)),
                  pl.BlockSpec((D_MODEL, nb), lambda j: (0, j)),
                  pl.BlockSpec((1, nb), lambda j: (0, j))],
        out_specs=pl.BlockSpec((rows, nb), lambda j: (0, j)),
        out_shape=jax.ShapeDtypeStruct((rows, n), F32),
        compiler_params=_params(1),
        name="ada",
    )(c, w_bf, b)


def _inproj_kernel(per_row, x_ref, mod_ref, g1_ref, wglu_ref, wq_ref, wk_ref, wv_ref, wqi_ref,
                   wkw_ref, wgc_ref, wga_ref, gq_ref, gk_ref, bd_ref,
                   glu_ref, k_ref, v_ref, kw_ref, qb_ref, kb_ref, vb_ref, qib_ref, sgc_ref, sga_ref):
    x = x_ref[...]
    shift = _mod(mod_ref, 0, per_row)
    scale = _mod(mod_ref, 1, per_row)
    rs = lax.rsqrt(jnp.mean(x * x, axis=-1, keepdims=True) + EPS)
    h = (x * rs) * g1_ref[...] * (1.0 + scale) + shift
    hb = h.astype(BF)

    def head_norm(z, g_ref):
        sq = z * z
        hi = sq.astype(BF)
        lo = (sq - hi.astype(F32)).astype(BF)
        ss = _dot(hi, bd_ref[...]) + _dot(lo, bd_ref[...])
        return z * lax.rsqrt(ss * (1.0 / HEAD_DIM) + EPS) * g_ref[...]

    gl = _dot(hb, wglu_ref[...])
    glu_ref[...] = gl[:, :CONV_DIM] * jax.nn.sigmoid(gl[:, CONV_DIM:])
    q = head_norm(_dot(hb, wq_ref[...]), gq_ref)
    k = head_norm(_dot(hb, wk_ref[...]), gk_ref)
    v = _dot(hb, wv_ref[...])
    k_ref[...] = k
    v_ref[...] = v
    qb_ref[...] = (q * HEAD_DIM ** -0.5).astype(BF)
    kb_ref[...] = k.astype(BF)
    vb_ref[...] = v.astype(BF)
    qib_ref[...] = (_dot(hb, wqi_ref[...]) * IDX_DIM ** -0.5).astype(BF)
    kw_ref[...] = _dot(hb, wkw_ref[...])
    sgc_ref[...] = jax.nn.sigmoid(_dot(hb, wgc_ref[...]))
    sga_ref[...] = jax.nn.sigmoid(_dot(hb, wga_ref[...]))


def _inproj(x, mod, per_row, rows_per_seq, tm, wts):
    t = x.shape[0]
    row = lambda n: pl.BlockSpec((tm, n), lambda i: (i, 0))
    out_cols = [(CONV_DIM, F32), (ATTN_DIM, F32), (ATTN_DIM, F32), (LANES, F32),
                (ATTN_DIM, BF), (ATTN_DIM, BF), (ATTN_DIM, BF), (ATTN_DIM, BF),
                (D_MODEL, F32), (D_MODEL, F32)]
    return pl.pallas_call(
        functools.partial(_inproj_kernel, per_row),
        grid=(t // tm,),
        in_specs=[row(D_MODEL), _mod_spec(per_row, tm, rows_per_seq)] + [_full(w.shape) for w in wts],
        out_specs=[row(n) for n, _ in out_cols],
        out_shape=[jax.ShapeDtypeStruct((t, n), dt) for n, dt in out_cols],
        compiler_params=_params(1),
        name="inproj",
    )(x, mod, *wts)


CONV_HALO = 32


def _conv_prompt_kernel(tt, glu_ref, wdw_ref, bdw_ref, y_ref, ext_ref):
    j = pl.program_id(1)

    @pl.when(j == 0)
    def _():
        ext_ref[0:CONV_HALO, :] = jnp.zeros((CONV_HALO, CONV_DIM), F32)
        ext_ref[CONV_HALO:, :] = glu_ref[0]

    n = tt + CONV_HALO
    t0 = pl.multiple_of(j * tt, tt)
    win = ext_ref[pl.ds(t0, n), :]
    acc = jnp.zeros((tt, CONV_DIM), F32) + bdw_ref[...]
    lead = CONV_HALO - CONV_STATE
    for r in range(SUBLANES):
        rolled = win if r == 0 else pltpu.roll(win, n - r, axis=0)
        for tap in range(CONV_WIDTH):
            off = lead + tap
            if off % SUBLANES == r:
                a = off - r
                acc = acc + wdw_ref[tap:tap + 1, :] * rolled[a:a + tt]
    y_ref[0] = acc


def _conv_prompt(glu, wdw, bdw, tt):
    b, s, _ = glu.shape
    return pl.pallas_call(
        functools.partial(_conv_prompt_kernel, tt),
        grid=(b, s // tt),
        in_specs=[pl.BlockSpec((1, s, CONV_DIM), lambda i, j: (i, 0, 0)),
                  _full(wdw.shape), _full(bdw.shape)],
        out_specs=pl.BlockSpec((1, tt, CONV_DIM), lambda i, j: (i, j, 0)),
        out_shape=jax.ShapeDtypeStruct((b, s, CONV_DIM), F32),
        scratch_shapes=[pltpu.VMEM((s + CONV_HALO, CONV_DIM), F32)],
        compiler_params=_params(2),
        name="conv_prompt",
    )(glu, wdw, bdw)


def _conv_sample_kernel(st_ref, glu_ref, wdw_ref, bdw_ref, y_ref):
    acc = bdw_ref[...] + wdw_ref[CONV_STATE:CONV_STATE + 1, :] * glu_ref[...]
    for j in range(CONV_STATE):
        acc = acc + wdw_ref[j:j + 1, :] * st_ref[j]
    y_ref[...] = acc


def _conv_sample(state_t, glu, wdw, bdw):
    db = glu.shape[0]
    return pl.pallas_call(
        _conv_sample_kernel,
        grid=(1,),
        in_specs=[_full(state_t.shape), _full(glu.shape), _full(wdw.shape), _full(bdw.shape)],
        out_specs=_full((db, CONV_DIM)),
        out_shape=jax.ShapeDtypeStruct((db, CONV_DIM), F32),
        compiler_params=_params(1),
        name="conv_sample",
    )(state_t, glu, wdw, bdw)


def _rel_bucket_np(dist):
    max_exact = REL_BUCKETS // 2
    d = np.maximum(dist, 0)
    df = np.maximum(d, 1).astype(np.float32)
    large = max_exact + (np.log(df / max_exact) / math.log(REL_MAX_DIST / max_exact)
                         * (REL_BUCKETS - max_exact)).astype(np.int32)
    return np.where(d < max_exact, d, np.minimum(large, REL_BUCKETS - 1)).astype(np.int32)


def _topk_threshold(count_ge, shape, kk):
    def bit_body(bi, thr_u):
        cand_u = thr_u | jnp.left_shift(jnp.int32(1), 31 - bi)
        cnt = count_ge(cand_u ^ INT_MIN)
        return jnp.where(cnt >= kk, cand_u, thr_u)

    thr_u = lax.fori_loop(0, 32, bit_body, jnp.zeros(shape, I32))
    return thr_u ^ INT_MIN


def _attn_prompt_kernel(tq, n_toep, kk, cb_ref, qh_ref, qih_ref, kw_ref, kT_ref, vh_ref, kiT_ref,
                        toep_ref, o_ref, keys_ref):
    tk = tq
    qt = pl.program_id(1)
    nk = qt + 1
    row = lax.broadcasted_iota(I32, (tq, tk), 0)
    col = lax.broadcasted_iota(I32, (tq, tk), 1)
    kwv = kw_ref[0]

    def score_body(kt, carry):
        kic = kiT_ref[0, kt]
        sc = jnp.zeros((tq, tk), F32)
        for h in range(IDX_HEADS):
            s = _dot(qih_ref[0, h], kic)
            w = kwv[:, IDX_DIM + h:IDX_DIM + h + 1] * IDX_HEADS ** -0.5
            sc = sc + jnp.maximum(s, 0.0) * w
        causal = (kt * tk + col) <= (qt * tq + row)
        keys_ref[kt] = jnp.where(causal, _monotone_key(sc), INT_MIN)
        return carry

    lax.fori_loop(0, nk, score_body, 0)

    def count_ge(cand):
        def cnt_body(kt, c):
            hit = jnp.where(keys_ref[kt] >= cand, 1, 0)
            for j in range(tk // LANES):
                c = c + hit[:, j * LANES:(j + 1) * LANES]
            return c
        c = lax.fori_loop(0, nk, cnt_body, jnp.zeros((tq, LANES), I32))
        return jnp.sum(c, axis=1, keepdims=True)

    thr = jnp.maximum(_topk_threshold(count_ge, (tq, 1), kk), INT_MIN + 1)

    for h in range(N_HEADS):
        q = qh_ref[0, h]
        cb = cb_ref[h]

        def kv_step(kt, carry, bias, thr_h):
            m, l, acc = carry
            s = _dot(q, kT_ref[0, h, kt]) + bias
            mask = keys_ref[kt] >= thr_h
            m_new = jnp.maximum(m, jnp.max(jnp.where(mask, s, NEG), axis=1, keepdims=True))
            p = jnp.where(mask, jnp.exp(s - m_new), 0.0)
            alpha = jnp.exp(m - m_new)
            l = alpha * l + jnp.sum(p, axis=1, keepdims=True)
            acc = alpha * acc + _dot(p.astype(BF), vh_ref[0, h, kt])
            return m_new, l, acc

        carry = (jnp.full((tq, 1), NEG, F32), jnp.zeros((tq, 1), F32), jnp.zeros((tq, HEAD_DIM), F32))
        carry = lax.fori_loop(0, jnp.maximum(nk - n_toep, 0),
                              lambda kt, c: kv_step(kt, c, cb, thr), carry)
        for o in reversed(range(n_toep)):
            thr_o = jnp.where(qt >= o, thr, INT_MAX)
            carry = kv_step(jnp.maximum(qt - o, 0), carry, toep_ref[o, h], thr_o)
        m, l, acc = carry
        o_ref[0, h] = acc / l


def _attn_prompt(cb, qh, qih, kw, kT, vh, kiT, toep, tq, kk):
    b, _, s, _ = qh.shape
    nq = s // tq
    n_toep = toep.shape[0]
    return pl.pallas_call(
        functools.partial(_attn_prompt_kernel, tq, n_toep, kk),
        grid=(b, nq),
        in_specs=[pl.BlockSpec(memory_space=pltpu.SMEM),
                  pl.BlockSpec((1, N_HEADS, tq, HEAD_DIM), lambda i, j: (i, 0, j, 0)),
                  pl.BlockSpec((1, IDX_HEADS, tq, IDX_DIM), lambda i, j: (i, 0, j, 0)),
                  pl.BlockSpec((1, tq, LANES), lambda i, j: (i, j, 0)),
                  pl.BlockSpec((1, N_HEADS, nq, HEAD_DIM, tq), lambda i, j: (i, 0, 0, 0, 0)),
                  pl.BlockSpec((1, N_HEADS, nq, tq, HEAD_DIM), lambda i, j: (i, 0, 0, 0, 0)),
                  pl.BlockSpec((1, nq, IDX_DIM, tq), lambda i, j: (i, 0, 0, 0)),
                  _full(toep.shape)],
        out_specs=pl.BlockSpec((1, N_HEADS, tq, HEAD_DIM), lambda i, j: (i, 0, j, 0)),
        out_shape=jax.ShapeDtypeStruct((b, N_HEADS, s, HEAD_DIM), F32),
        scratch_shapes=[pltpu.VMEM((nq, tq, tq), I32)],
        compiler_params=_params(2),
        name="attn_prompt",
    )(cb, qh, qih, kw, kT, vh, kiT, toep)


PAGES_PER_STEP = 8


def _page_specs(block, n_pages):
    def spec(r):
        return pl.BlockSpec(block, lambda b, g, pt: (pt[b * n_pages + g * PAGES_PER_STEP + r],) + (0,) * (len(block) - 1))
    return [spec(r) for r in range(PAGES_PER_STEP)]


def _samp_scores_kernel(pt_ref, qi_ref, wi_ref, *refs):
    pages, o_ref = refs[:PAGES_PER_STEP], refs[PAGES_PER_STEP]
    qi = qi_ref[0]
    w = wi_ref[0] * IDX_HEADS ** -0.5
    rows = []
    for r in range(PAGES_PER_STEP):
        s = _dot_nt(qi, pages[r][0].astype(BF))
        rows.append(jnp.sum(jnp.maximum(s, 0.0) * w, axis=0, keepdims=True))
    o_ref[0, 0] = jnp.concatenate(rows, axis=0)


def _samp_scores(page_table_flat, qi, wi, cache_idx, n_pages):
    db = qi.shape[0]
    ng = n_pages // PAGES_PER_STEP
    grid_spec = pltpu.PrefetchScalarGridSpec(
        num_scalar_prefetch=1,
        grid=(db, ng),
        in_specs=[pl.BlockSpec((1, IDX_HEADS, IDX_DIM), lambda b, g, pt: (b, 0, 0)),
                  pl.BlockSpec((1, IDX_HEADS, 1), lambda b, g, pt: (b, 0, 0))]
                 + _page_specs((1, PAGE_SIZE, IDX_DIM), n_pages),
        out_specs=pl.BlockSpec((1, 1, PAGES_PER_STEP, PAGE_SIZE), lambda b, g, pt: (b, g, 0, 0)),
    )
    return pl.pallas_call(
        _samp_scores_kernel,
        grid_spec=grid_spec,
        out_shape=jax.ShapeDtypeStruct((db, ng, PAGES_PER_STEP, PAGE_SIZE), F32),
        compiler_params=_params(2),
        name="samp_scores",
    )(page_table_flat, qi, wi, *([cache_idx] * PAGES_PER_STEP))


def _samp_select_kernel(kk, sc_ref, qi_ref, ki8_ref, kw_ref, keys_ref, thr_ref, knew_ref):
    prod = qi_ref[...].astype(F32) * ki8_ref[...].astype(F32)
    kwv = kw_ref[...]
    s_new = jnp.zeros((prod.shape[0], 1), F32)
    for h in range(IDX_HEADS):
        s = jnp.sum(prod[:, h * IDX_DIM:(h + 1) * IDX_DIM], axis=1, keepdims=True)
        w = kwv[:, IDX_DIM + h:IDX_DIM + h + 1] * IDX_HEADS ** -0.5
        s_new = s_new + jnp.maximum(s, 0.0) * w
    key_new = _monotone_key(s_new)
    keys_ref[...] = _monotone_key(sc_ref[...])

    def count_ge(cand):
        hit = jnp.where(keys_ref[...] >= cand, 1, 0)
        return jnp.sum(hit, axis=1, keepdims=True) + jnp.where(key_new >= cand, 1, 0)

    thr = _topk_threshold(count_ge, key_new.shape, kk)
    thr_ref[...] = jnp.broadcast_to(thr, thr_ref.shape)
    knew_ref[...] = jnp.broadcast_to(key_new, knew_ref.shape)


def _samp_select(scores, qib, ki8, kw, kk):
    db, past = scores.shape
    return pl.pallas_call(
        functools.partial(_samp_select_kernel, kk),
        grid=(1,),
        in_specs=[_full(scores.shape), _full(qib.shape), _full(ki8.shape), _full(kw.shape)],
        out_specs=[_full((db, past)), _full((db, LANES)), _full((db, LANES))],
        out_shape=[jax.ShapeDtypeStruct((db, past), I32),
                   jax.ShapeDtypeStruct((db, LANES), I32),
                   jax.ShapeDtypeStruct((db, LANES), I32)],
        compiler_params=_params(1),
        name="samp_select",
    )(scores, qib, ki8, kw)


def _samp_attn_kernel(pt_ref, a_ref, keys_ref, thr_ref, knew_ref, bias_ref, kn_ref, vn_ref, b0_ref, *refs):
    k_pages = refs[:PAGES_PER_STEP]
    v_pages = refs[PAGES_PER_STEP:2 * PAGES_PER_STEP]
    o_ref, m_ref, l_ref, acc_ref = refs[2 * PAGES_PER_STEP:]
    g = pl.program_id(1)

    @pl.when(g == 0)
    def _():
        m_ref[...] = jnp.full(m_ref.shape, NEG, F32)
        l_ref[...] = jnp.zeros(l_ref.shape, F32)
        acc_ref[...] = jnp.zeros(acc_ref.shape, F32)

    a = a_ref[0]
    thr = thr_ref[0]

    def update(s, mask, pv):
        m = m_ref[...]
        m_new = jnp.maximum(m, jnp.max(jnp.where(mask, s, NEG), axis=1, keepdims=True))
        p = jnp.where(mask, jnp.exp(s - m_new), 0.0)
        alpha = jnp.exp(m - m_new)
        l_ref[...] = alpha * l_ref[...] + jnp.sum(p, axis=1, keepdims=True)
        acc_ref[...] = alpha * acc_ref[...] + pv(p)
        m_ref[...] = m_new

    for r in range(PAGES_PER_STEP):
        s = _dot_nt(a, k_pages[r][0].astype(BF)) + bias_ref[0, r]
        mask = keys_ref[0, 0, r:r + 1, :] >= thr
        update(s, mask, lambda p, r=r: _dot(p.astype(BF), v_pages[r][0].astype(BF)))

    @pl.when(g == pl.num_programs(1) - 1)
    def _():
        kn = kn_ref[0].astype(BF).astype(F32)
        vn = vn_ref[0].astype(BF).astype(F32)
        s = jnp.sum(a.astype(F32) * kn, axis=1, keepdims=True) + b0_ref[...]
        mask = knew_ref[0][:, :1] >= thr[:, :1]
        update(s, mask, lambda p: p * vn)
        out = acc_ref[...] / l_ref[...]
        hrow = lax.broadcasted_iota(I32, out.shape, 0)
        hcol = lax.broadcasted_iota(I32, out.shape, 1) // HEAD_DIM
        o_ref[0] = jnp.sum(jnp.where(hrow == hcol, out, 0.0), axis=0, keepdims=True)


def _samp_attn(page_table_flat, a, keys, thr, knew, bias_t, kn, vn, b0, cache_k, cache_v, n_pages):
    db = a.shape[0]
    ng = n_pages // PAGES_PER_STEP
    per_seq = lambda shape: pl.BlockSpec((1,) + shape, lambda b, g, pt: (b,) + (0,) * len(shape))
    grid_spec = pltpu.PrefetchScalarGridSpec(
        num_scalar_prefetch=1,
        grid=(db, ng),
        in_specs=[per_seq((N_HEADS, ATTN_DIM)),
                  pl.BlockSpec((1, 1, PAGES_PER_STEP, PAGE_SIZE), lambda b, g, pt: (b, g, 0, 0)),
                  per_seq((1, LANES)), per_seq((1, LANES)),
                  pl.BlockSpec((1, PAGES_PER_STEP, N_HEADS, PAGE_SIZE), lambda b, g, pt: (g, 0, 0, 0)),
                  per_seq((1, ATTN_DIM)), per_seq((1, ATTN_DIM)),
                  pl.BlockSpec((N_HEADS, 1), lambda b, g, pt: (0, 0))]
                 + _page_specs((1, PAGE_SIZE, ATTN_DIM), n_pages)
                 + _page_specs((1, PAGE_SIZE, ATTN_DIM), n_pages),
        out_specs=per_seq((1, ATTN_DIM)),
        scratch_shapes=[pltpu.VMEM((N_HEADS, 1), F32), pltpu.VMEM((N_HEADS, 1), F32),
                        pltpu.VMEM((N_HEADS, ATTN_DIM), F32)],
    )
    return pl.pallas_call(
        _samp_attn_kernel,
        grid_spec=grid_spec,
        out_shape=jax.ShapeDtypeStruct((db, 1, ATTN_DIM), F32),
        compiler_params=_params(2),
        name="samp_attn",
    )(page_table_flat, a, keys, thr, knew, bias_t, kn, vn, b0,
      *([cache_k] * PAGES_PER_STEP), *([cache_v] * PAGES_PER_STEP))


def _merge_kernel(per_row, x_ref, yc_ref, at_ref, sgc_ref, sga_ref, mod_ref, lng_ref, lnb_ref,
                  wco_ref, wao_ref, wout_ref, g2_ref, wpq_ref, kbd_ref,
                  xm_ref, h2t_ref, st_ref):
    yc = yc_ref[...]
    mu = jnp.mean(yc, axis=-1, keepdims=True)
    yd = yc - mu
    ln = yd * lax.rsqrt(jnp.mean(yd * yd, axis=-1, keepdims=True) + EPS) * lng_ref[...] + lnb_ref[...]
    act = ln * jax.nn.sigmoid(ln)
    conv_out = _dot(act.astype(BF), wco_ref[...])
    attn_out = _dot(at_ref[...].astype(BF), wao_ref[...])
    merged = sgc_ref[...] * conv_out + sga_ref[...] * attn_out
    xm = x_ref[...] + _mod(mod_ref, 2, per_row) * _dot(merged.astype(BF), wout_ref[...])
    xm_ref[...] = xm
    rs = lax.rsqrt(jnp.mean(xm * xm, axis=-1, keepdims=True) + EPS)
    h2 = (xm * rs) * g2_ref[...] * (1.0 + _mod(mod_ref, 4, per_row)) + _mod(mod_ref, 3, per_row)
    h2t_ref[...] = h2.T.astype(BF)
    pq = _dot(h2.astype(BF), wpq_ref[...])
    st_ref[...] = _dot_nt(kbd_ref[...], pq.astype(BF))


def _merge(x, yc, at, sgc, sga, mod, per_row, rows_per_seq, tm, wts):
    t = x.shape[0]
    row = lambda n: pl.BlockSpec((tm, n), lambda i: (i, 0))
    col = lambda n: pl.BlockSpec((n, tm), lambda i: (0, i))
    n_scores = PEER_HEADS * 2 * PEER_KEYS
    return pl.pallas_call(
        functools.partial(_merge_kernel, per_row),
        grid=(t // tm,),
        in_specs=[row(D_MODEL), row(CONV_DIM), row(ATTN_DIM), row(D_MODEL), row(D_MODEL),
                  _mod_spec(per_row, tm, rows_per_seq)] + [_full(w.shape) for w in wts],
        out_specs=[row(D_MODEL), col(D_MODEL), col(n_scores)],
        out_shape=[jax.ShapeDtypeStruct((t, D_MODEL), F32),
                   jax.ShapeDtypeStruct((D_MODEL, t), BF),
                   jax.ShapeDtypeStruct((n_scores, t), F32)],
        compiler_params=_params(1),
        name="merge",
    )(x, yc, at, sgc, sga, mod, *wts)


N_EXTRACT = PEER_TOPK + 1


def _top_extract(cur, n):
    outs = []
    for _ in range(n):
        mx = jnp.max(cur, axis=0, keepdims=True)
        outs.append(mx)
        cur = jnp.where(cur >= mx, -jnp.inf, cur)
    return outs


def _peer_select_kernel(st_ref, s1_ref, e1_ref, tau_ref, al_ref):
    tl = st_ref.shape[1]

    def head_body(h, carry):
        base = pl.multiple_of(h * 2 * PEER_KEYS, 2 * PEER_KEYS)
        s0 = st_ref[pl.ds(base, PEER_KEYS), :]
        s1 = st_ref[pl.ds(base + PEER_KEYS, PEER_KEYS), :]
        a0 = _top_extract(s0, N_EXTRACT)
        a1 = _top_extract(s1, N_EXTRACT)
        pad = [jnp.full((1, tl), -jnp.inf, F32)] * (3 * SUBLANES - N_EXTRACT)
        a1_all = jnp.concatenate(a1 + pad, axis=0)
        cands = [a0[0] + a1_all] + [a0[k] + a1_all[:SUBLANES] for k in range(1, N_EXTRACT)]
        top = _top_extract(jnp.concatenate(cands, axis=0), N_EXTRACT)
        thr = 0.5 * (top[PEER_TOPK - 1] + top[PEER_TOPK])
        z = jnp.zeros_like(thr)
        for k in range(PEER_TOPK):
            z = z + jnp.exp(top[k] - top[0])
        s1_ref[h] = s1
        e1_ref[h] = jnp.exp(s1 - a1[0])
        tau_ref[h] = thr - s0
        al_ref[h] = jnp.exp(s0 - a0[0]) / z
        return carry

    lax.fori_loop(0, PEER_HEADS, head_body, 0)


def _peer_select(st, tl):
    t = st.shape[1]
    spec = pl.BlockSpec((PEER_HEADS, PEER_KEYS, tl), lambda i: (0, 0, i))
    shp = jax.ShapeDtypeStruct((PEER_HEADS, PEER_KEYS, t), F32)
    return pl.pallas_call(
        _peer_select_kernel,
        grid=(t // tl,),
        in_specs=[pl.BlockSpec((st.shape[0], tl), lambda i: (0, i))],
        out_specs=[spec] * 4,
        out_shape=[shp] * 4,
        compiler_params=_params(1),
        name="peer_select",
    )(st)


PEER_ROWS_PER_STEP = SUBLANES


def _peer_main_kernel(per_row, h2t_ref, u_ref, vt_ref, s1_ref, e1_ref, tau_ref, al_ref, xm_ref, mod_ref,
                      o_ref, acc_ref, g_ref):
    j = pl.program_id(1)
    tm = h2t_ref.shape[1]
    lb = min(LANES, tm)

    @pl.when(j == 0)
    def _():
        acc_ref[...] = jnp.zeros(acc_ref.shape, F32)

    a_t = _dot(u_ref[...], h2t_ref[...])
    i0 = pl.multiple_of(j * PEER_ROWS_PER_STEP, PEER_ROWS_PER_STEP)
    for ii in range(PEER_ROWS_PER_STEP):
        for tb in range(tm // lb):
            ts = slice(tb * lb, (tb + 1) * lb)
            w = jnp.zeros((PEER_KEYS, lb), F32)
            for h in range(PEER_HEADS):
                tau = tau_ref[h, pl.ds(i0, PEER_ROWS_PER_STEP), ts][ii:ii + 1]
                al = al_ref[h, pl.ds(i0, PEER_ROWS_PER_STEP), ts][ii:ii + 1]
                w = w + jnp.where(s1_ref[h, :, ts] >= tau, e1_ref[h, :, ts] * al, 0.0)
            a = a_t[ii * PEER_KEYS:(ii + 1) * PEER_KEYS, ts]
            gelu = 0.5 * a * (1.0 + lax.erf(a * (2.0 ** -0.5)))
            g_ref[ii * PEER_KEYS:(ii + 1) * PEER_KEYS, ts] = (w * gelu).astype(BF)
    acc_ref[...] += _dot(vt_ref[...], g_ref[...])

    @pl.when(j == pl.num_programs(1) - 1)
    def _():
        o_ref[...] = xm_ref[...] + _mod(mod_ref, 5, per_row) * acc_ref[...].T


def _peer_main(h2t, u_bf, vt_bf, s1, e1, tau, al, xm, mod, per_row, rows_per_seq, tm):
    t = xm.shape[0]
    ec = PEER_ROWS_PER_STEP * PEER_KEYS
    n_e = u_bf.shape[0]
    sel = pl.BlockSpec((PEER_HEADS, PEER_KEYS, tm), lambda i, j: (0, 0, i))
    if per_row:
        mod_spec = pl.BlockSpec((6, tm, D_MODEL), lambda i, j: (0, i, 0))
    else:
        tiles_per_seq = rows_per_seq // tm
        mod_spec = pl.BlockSpec((1, 6, D_MODEL), lambda i, j: (i // tiles_per_seq, 0, 0))
    return pl.pallas_call(
        functools.partial(_peer_main_kernel, per_row),
        grid=(t // tm, n_e // ec),
        in_specs=[pl.BlockSpec((D_MODEL, tm), lambda i, j: (0, i)),
                  pl.BlockSpec((ec, D_MODEL), lambda i, j: (j, 0)),
                  pl.BlockSpec((D_MODEL, ec), lambda i, j: (0, j)),
                  sel, sel, sel, sel,
                  pl.BlockSpec((tm, D_MODEL), lambda i, j: (i, 0)),
                  mod_spec],
        out_specs=pl.BlockSpec((tm, D_MODEL), lambda i, j: (i, 0)),
        out_shape=jax.ShapeDtypeStruct((t, D_MODEL), F32),
        scratch_shapes=[pltpu.VMEM((D_MODEL, tm), F32), pltpu.VMEM((ec, tm), BF)],
        compiler_params=_params(2),
        name="peer_main",
    )(h2t, u_bf, vt_bf, s1, e1, tau, al, xm, mod)


def _head_major(a, b, s):
    return a.reshape(b, s, N_HEADS, -1).transpose(0, 2, 1, 3)


def _tile(n, pref):
    return pref if n % pref == 0 else n


def kernel(x_prompt, x_sample, cache_k, cache_v, cache_idx_k, state_conv, page_table, c_prompt, c_sample,
           w_ada, b_ada, norm1_g, w_in, q_norm_g, k_norm_g, rel_bias, w_dw, b_dw, conv_ln_g, conv_ln_b,
           w_conv_o, w_attn_o, w_out, norm2_g, w_peer_q, peer_keys, peer_u, peer_v):
    b, s, _ = x_prompt.shape
    db, ds, _ = x_sample.shape
    n_pages = page_table.shape[1]
    past = n_pages * PAGE_SIZE
    assert w_ada.shape[0] == 1, "one layer"
    assert ds == 1, "one new token per sample sequence"
    assert n_pages % PAGES_PER_STEP == 0
    tp = b * s

    w = w_in[0]
    pts = np.cumsum([0, 2 * CONV_DIM, ATTN_DIM, ATTN_DIM, ATTN_DIM, IDX_HEADS * IDX_DIM, IDX_DIM, IDX_HEADS,
                     D_MODEL, D_MODEL])
    seg = lambda i: w[:, pts[i]:pts[i + 1]].astype(BF)
    w_kw = jnp.pad(w[:, pts[5]:pts[7]], ((0, 0), (0, LANES - IDX_DIM - IDX_HEADS))).astype(BF)
    head_of = np.arange(ATTN_DIM) // HEAD_DIM
    bd = jnp.asarray(head_of[:, None] == head_of[None, :], BF)
    row = lambda v: v.reshape(1, -1)
    in_wts = [row(norm1_g[0]), seg(0), seg(1), seg(2), seg(3), seg(4), w_kw, seg(7), seg(8),
              row(jnp.tile(q_norm_g[0], N_HEADS)), row(jnp.tile(k_norm_g[0], N_HEADS)), bd]
    n_half = PEER_HEADS * 2
    kb16 = peer_keys[0].reshape(n_half, PEER_KEYS, PEER_HALF)
    kbd_t = (jnp.eye(n_half, dtype=F32)[:, None, :, None] * kb16[:, :, None, :]).reshape(
        n_half * PEER_KEYS, n_half * PEER_HALF).astype(BF)
    merge_wts = [row(conv_ln_g[0]), row(conv_ln_b[0]), w_conv_o[0].astype(BF), w_attn_o[0].astype(BF),
                 w_out[0].astype(BF), row(norm2_g[0]), w_peer_q[0].astype(BF), kbd_t]
    u_bf = peer_u[0].astype(BF)
    vt_bf = peer_v[0].T.astype(BF)

    mod = _ada(jnp.concatenate([c_prompt, c_sample], axis=0), w_ada[0].astype(BF), row(b_ada[0]))
    mod_p = mod[:b].reshape(b, 6, D_MODEL)
    mod_s = mod[b:].reshape(db, 6, D_MODEL).transpose(1, 0, 2)

    tm_p = _tile(s, 256)
    glu_p, k_p, v_p, kw_p, qb_p, kb_p, vb_p, qib_p, sgc_p, sga_p = _inproj(
        x_prompt.reshape(tp, D_MODEL), mod_p, False, s, tm_p, in_wts)
    glu_s, k_s, v_s, kw_s, qb_s, kb_s, vb_s, qib_s, sgc_s, sga_s = _inproj(
        x_sample.reshape(db, D_MODEL), mod_s, True, 1, db, in_wts)

    yc_p = _conv_prompt(glu_p.reshape(b, s, CONV_DIM), w_dw[0], row(b_dw[0]), _tile(s, 256)).reshape(tp, CONV_DIM)
    yc_s = _conv_sample(state_conv[0].transpose(1, 0, 2), glu_s, w_dw[0], row(b_dw[0]))

    tq = _tile(s, 256)
    nq = s // tq
    bucket = _rel_bucket_np(np.arange(max(s, past + 1)))
    n_toep = nq
    while n_toep > 0 and len(set(bucket[max((n_toep - 1) * tq - (tq - 1), 0):n_toep * tq])) == 1 \
            and bucket[(n_toep - 1) * tq] == bucket[s - 1]:
        n_toep -= 1
    n_toep = max(n_toep, 1)
    ii = np.arange(tq)
    dist = np.arange(n_toep)[:, None, None] * tq + ii[None, :, None] - ii[None, None, :]
    toep = rel_bias[_rel_bucket_np(dist)].transpose(0, 3, 1, 2)
    cb = rel_bias[int(bucket[s - 1])]

    kk_p = min(IDX_TOPK_MAX, s // 4)
    qh = _head_major(qb_p, b, s)
    qih = _head_major(qib_p, b, s)
    kT = kb_p.reshape(b, nq, tq, N_HEADS, HEAD_DIM).transpose(0, 3, 1, 4, 2)
    vh = vb_p.reshape(b, nq, tq, N_HEADS, HEAD_DIM).transpose(0, 3, 1, 2, 4)
    kiT = kw_p[:, :IDX_DIM].astype(BF).reshape(b, nq, tq, IDX_DIM).transpose(0, 1, 3, 2)
    at_p = _attn_prompt(cb, qh, qih, kw_p.reshape(b, s, LANES), kT, vh, kiT, toep, tq, kk_p)
    at_p = at_p.transpose(0, 2, 1, 3).reshape(tp, ATTN_DIM)

    kk_s = min(IDX_TOPK_MAX, (past + ds) // 4)
    pt_flat = page_table.reshape(-1)
    sc = _samp_scores(pt_flat, qib_s.reshape(db, IDX_HEADS, IDX_DIM),
                      kw_s[:, IDX_DIM:IDX_DIM + IDX_HEADS].reshape(db, IDX_HEADS, 1),
                      cache_idx_k[0], n_pages)
    ki8 = jnp.tile(kw_s[:, :IDX_DIM].astype(BF), (1, IDX_HEADS))
    keys_s, thr_s, knew_s = _samp_select(sc.reshape(db, past), qib_s, ki8, kw_s, kk_s)
    a_bd = (qb_s.reshape(db, N_HEADS, 1, HEAD_DIM)
            * jnp.eye(N_HEADS, dtype=BF)[None, :, :, None]).reshape(db, N_HEADS, ATTN_DIM)
    pos = np.arange(past).reshape(n_pages // PAGES_PER_STEP, PAGES_PER_STEP, PAGE_SIZE)
    bias_t = rel_bias[_rel_bucket_np(past - pos)].transpose(0, 1, 3, 2)
    b0 = rel_bias[int(_rel_bucket_np(np.zeros((), np.int64)))].reshape(N_HEADS, 1)
    n_phys = cache_k.shape[1]
    at_s = _samp_attn(pt_flat, a_bd, keys_s.reshape(sc.shape), thr_s.reshape(db, 1, LANES),
                      knew_s.reshape(db, 1, LANES), bias_t, k_s.reshape(db, 1, ATTN_DIM),
                      v_s.reshape(db, 1, ATTN_DIM), b0,
                      cache_k[0].reshape(n_phys, PAGE_SIZE, ATTN_DIM),
                      cache_v[0].reshape(n_phys, PAGE_SIZE, ATTN_DIM), n_pages).reshape(db, ATTN_DIM)

    def tail(x, yc, at, sgc, sga, mod_g, per_row, rows_per_seq, tm_merge, tl_sel, tm_peer):
        xm, h2t, st = _merge(x, yc, at, sgc, sga, mod_g, per_row, rows_per_seq, tm_merge, merge_wts)
        s1, e1, tau, al = _peer_select(st, tl_sel)
        return _peer_main(h2t, u_bf, vt_bf, s1, e1, tau, al, xm, mod_g, per_row, rows_per_seq, tm_peer)

    y_p = tail(x_prompt.reshape(tp, D_MODEL), yc_p, at_p, sgc_p, sga_p, mod_p, False, s,
               _tile(s, 256), _tile(tp, 256), _tile(s, 512))
    y_s = tail(x_sample.reshape(db, D_MODEL), yc_s, at_s, sgc_s, sga_s, mod_s, True, 1, db, db, db)

    glu_p3 = glu_p.reshape(b, s, CONV_DIM)
    conv_s = jnp.concatenate([state_conv[0], glu_s[:, None, :]], axis=1)[:, -CONV_STATE:]
    return (y_p.reshape(b, s, D_MODEL), y_s.reshape(db, ds, D_MODEL),
            k_p.reshape(1, b, s, N_HEADS, HEAD_DIM), v_p.reshape(1, b, s, N_HEADS, HEAD_DIM),
            kw_p[:, :IDX_DIM].reshape(1, b, s, IDX_DIM), glu_p3[None, :, -CONV_STATE:],
            k_s.reshape(1, db, ds, N_HEADS, HEAD_DIM), v_s.reshape(1, db, ds, N_HEADS, HEAD_DIM),
            kw_s[:, :IDX_DIM].reshape(1, db, ds, IDX_DIM), conv_s[None])
```

```python
import functools
import math

import numpy as np
import jax
import jax.numpy as jnp
from jax import lax
from jax.experimental import pallas as pl
from jax.experimental.pallas import tpu as pltpu

F32 = jnp.float32
BF = jnp.bfloat16
I32 = jnp.int32

D_MODEL = 1024
N_HEADS = 8
HEAD_DIM = 64
ATTN_DIM = N_HEADS * HEAD_DIM
IDX_HEADS = 8
IDX_DIM = 64
IDX_TOPK_MAX = 256
REL_BUCKETS = 32
REL_MAX_DIST = 128
CONV_DIM = 512
CONV_WIDTH = 31
CONV_STATE = CONV_WIDTH - 1
PEER_HEADS = 8
PEER_KEYS = 128
PEER_HALF = 64
PEER_TOPK = 16
PAGE_SIZE = 128
EPS = 1e-6

NEG = -1e30
INT_MIN = -(2 ** 31)
INT_MAX = 2 ** 31 - 1
LANES = 128
SUBLANES = 8
VMEM_LIMIT = 52 * 1024 * 1024


def _params(n_axes):
    return pltpu.CompilerParams(dimension_semantics=("arbitrary",) * n_axes,
                                vmem_limit_bytes=VMEM_LIMIT)


def _full(shape):
    zeros = (0,) * len(shape)
    return pl.BlockSpec(shape, lambda *_: zeros)


def _mod(mod_ref, k, per_row):
    return mod_ref[k] if per_row else mod_ref[0, k:k + 1, :]


def _mod_spec(per_row, tm, rows_per_seq):
    if per_row:
        return pl.BlockSpec((6, tm, D_MODEL), lambda i: (0, i, 0))
    tiles_per_seq = rows_per_seq // tm
    return pl.BlockSpec((1, 6, D_MODEL), lambda i: (i // tiles_per_seq, 0, 0))


def _dot(a, b):
    return jnp.dot(a, b, preferred_element_type=F32)


def _dot_nt(a, b):
    return lax.dot_general(a, b, (((1,), (1,)), ((), ())), preferred_element_type=F32)


def _monotone_key(x):
    bits = lax.bitcast_convert_type(x, I32)
    return bits ^ ((bits >> 31) & INT_MAX)


def _ada_kernel(c_ref, w_ref, b_ref, o_ref):
    c = c_ref[...]
    a = (c * jax.nn.sigmoid(c)).astype(BF)
    o_ref[...] = _dot(a, w_ref[...]) + b_ref[...]


def _ada(c, w_bf, b):
    rows = c.shape[0]
    n = w_bf.shape[1]
    nb = 1536
    return pl.pallas_call(
        _ada_kernel,
        grid=(n // nb,),
        in_specs=[_full((rows, D_MODEL)),
                  pl.BlockSpec((D_MODEL, nb), lambda j: (0, j)),
                  pl.BlockSpec((1, nb), lambda j: (0, j))],
        out_specs=pl.BlockSpec((rows, nb), lambda j: (0, j)),
        out_shape=jax.ShapeDtypeStruct((rows, n), F32),
        compiler_params=_params(1),
        name="ada",
    )(c, w_bf, b)


def _inproj_kernel(per_row, x_ref, mod_ref, g1_ref, wglu_ref, wq_ref, wk_ref, wv_ref, wqi_ref,
                   wkw_ref, wgc_ref, wga_ref, gq_ref, gk_ref, bd_ref,
                   glu_ref, k_ref, v_ref, kw_ref, qb_ref, kb_ref, vb_ref, qib_ref, sgc_ref, sga_ref):
    x = x_ref[...]
    shift = _mod(mod_ref, 0, per_row)
    scale = _mod(mod_ref, 1, per_row)
    rs = lax.rsqrt(jnp.mean(x * x, axis=-1, keepdims=True) + EPS)
    h = (x * rs) * g1_ref[...] * (1.0 + scale) + shift
    hb = h.astype(BF)

    def head_norm(z, g_ref):
        sq = z * z
        hi = sq.astype(BF)
        lo = (sq - hi.astype(F32)).astype(BF)
        ss = _dot(hi, bd_ref[...]) + _dot(lo, bd_ref[...])
        return z * lax.rsqrt(ss * (1.0 / HEAD_DIM) + EPS) * g_ref[...]

    gl = _dot(hb, wglu_ref[...])
    glu_ref[...] = gl[:, :CONV_DIM] * jax.nn.sigmoid(gl[:, CONV_DIM:])
    q = head_norm(_dot(hb, wq_ref[...]), gq_ref)
    k = head_norm(_dot(hb, wk_ref[...]), gk_ref)
    v = _dot(hb, wv_ref[...])
    k_ref[...] = k
    v_ref[...] = v
    qb_ref[...] = (q * HEAD_DIM ** -0.5).astype(BF)
    kb_ref[...] = k.astype(BF)
    vb_ref[...] = v.astype(BF)
    qib_ref[...] = (_dot(hb, wqi_ref[...]) * IDX_DIM ** -0.5).astype(BF)
    kw_ref[...] = _dot(hb, wkw_ref[...])
    sgc_ref[...] = jax.nn.sigmoid(_dot(hb, wgc_ref[...]))
    sga_ref[...] = jax.nn.sigmoid(_dot(hb, wga_ref[...]))


def _inproj(x, mod, per_row, rows_per_seq, tm, wts):
    t = x.shape[0]
    row = lambda n: pl.BlockSpec((tm, n), lambda i: (i, 0))
    out_cols = [(CONV_DIM, F32), (ATTN_DIM, F32), (ATTN_DIM, F32), (LANES, F32),
                (ATTN_DIM, BF), (ATTN_DIM, BF), (ATTN_DIM, BF), (ATTN_DIM, BF),
                (D_MODEL, F32), (D_MODEL, F32)]
    return pl.pallas_call(
        functools.partial(_inproj_kernel, per_row),
        grid=(t // tm,),
        in_specs=[row(D_MODEL), _mod_spec(per_row, tm, rows_per_seq)] + [_full(w.shape) for w in wts],
        out_specs=[row(n) for n, _ in out_cols],
        out_shape=[jax.ShapeDtypeStruct((t, n), dt) for n, dt in out_cols],
        compiler_params=_params(1),
        name="inproj",
    )(x, mod, *wts)


CONV_HALO = 32


def _conv_prompt_kernel(tt, glu_ref, wdw_ref, bdw_ref, y_ref, ext_ref):
    j = pl.program_id(1)

    @pl.when(j == 0)
    def _():
        ext_ref[0:CONV_HALO, :] = jnp.zeros((CONV_HALO, CONV_DIM), F32)
        ext_ref[CONV_HALO:, :] = glu_ref[0]

    n = tt + CONV_HALO
    t0 = pl.multiple_of(j * tt, tt)
    win = ext_ref[pl.ds(t0, n), :]
    acc = jnp.zeros((tt, CONV_DIM), F32) + bdw_ref[...]
    lead = CONV_HALO - CONV_STATE
    for r in range(SUBLANES):
        rolled = win if r == 0 else pltpu.roll(win, n - r, axis=0)
        for tap in range(CONV_WIDTH):
            off = lead + tap
            if off % SUBLANES == r:
                a = off - r
                acc = acc + wdw_ref[tap:tap + 1, :] * rolled[a:a + tt]
    y_ref[0] = acc


def _conv_prompt(glu, wdw, bdw, tt):
    b, s, _ = glu.shape
    return pl.pallas_call(
        functools.partial(_conv_prompt_kernel, tt),
        grid=(b, s // tt),
        in_specs=[pl.BlockSpec((1, s, CONV_DIM), lambda i, j: (i, 0, 0)),
                  _full(wdw.shape), _full(bdw.shape)],
        out_specs=pl.BlockSpec((1, tt, CONV_DIM), lambda i, j: (i, j, 0)),
        out_shape=jax.ShapeDtypeStruct((b, s, CONV_DIM), F32),
        scratch_shapes=[pltpu.VMEM((s + CONV_HALO, CONV_DIM), F32)],
        compiler_params=_params(2),
        name="conv_prompt",
    )(glu, wdw, bdw)


def _conv_sample_kernel(st_ref, glu_ref, wdw_ref, bdw_ref, y_ref):
    acc = bdw_ref[...] + wdw_ref[CONV_STATE:CONV_STATE + 1, :] * glu_ref[...]
    for j in range(CONV_STATE):
        acc = acc + wdw_ref[j:j + 1, :] * st_ref[j]
    y_ref[...] = acc


def _conv_sample(state_t, glu, wdw, bdw):
    db = glu.shape[0]
    return pl.pallas_call(
        _conv_sample_kernel,
        grid=(1,),
        in_specs=[_full(state_t.shape), _full(glu.shape), _full(wdw.shape), _full(bdw.shape)],
        out_specs=_full((db, CONV_DIM)),
        out_shape=jax.ShapeDtypeStruct((db, CONV_DIM), F32),
        compiler_params=_params(1),
        name="conv_sample",
    )(state_t, glu, wdw, bdw)


def _rel_bucket_np(dist):
    max_exact = REL_BUCKETS // 2
    d = np.maximum(dist, 0)
    df = np.maximum(d, 1).astype(np.float32)
    large = max_exact + (np.log(df / max_exact) / math.log(REL_MAX_DIST / max_exact)
                         * (REL_BUCKETS - max_exact)).astype(np.int32)
    return np.where(d < max_exact, d, np.minimum(large, REL_BUCKETS - 1)).astype(np.int32)


def _topk_threshold(count_ge, shape, kk):
    def bit_body(bi, thr_u):
        cand_u = thr_u | jnp.left_shift(jnp.int32(1), 31 - bi)
        cnt = count_ge(cand_u ^ INT_MIN)
        return jnp.where(cnt >= kk, cand_u, thr_u)

    thr_u = lax.fori_loop(0, 32, bit_body, jnp.zeros(shape, I32))
    return thr_u ^ INT_MIN


def _attn_prompt_kernel(tq, n_toep, kk, cb_ref, qT_ref, qiT_ref, wiT_ref, kh_ref, vT_ref, ki_ref, toep_ref,
                        o_ref, keys_ref, m_ref, l_ref, acc_ref):
    tk = tq
    qt = pl.program_id(1)
    nk = qt + 1
    krow = lax.broadcasted_iota(I32, (tk, tq), 0)
    qcol = lax.broadcasted_iota(I32, (tk, tq), 1)

    def score_body(kt, carry):
        kic = ki_ref[0, kt]
        sc = jnp.zeros((tk, tq), F32)
        for h in range(IDX_HEADS):
            s = _dot(kic, qiT_ref[0, h])
            sc = sc + jnp.maximum(s, 0.0) * (wiT_ref[0, h:h + 1, :] * IDX_HEADS ** -0.5)
        causal = (kt * tk + krow) <= (qt * tq + qcol)
        keys_ref[kt] = jnp.where(causal, _monotone_key(sc), INT_MIN)
        return carry

    lax.fori_loop(0, nk, score_body, 0)

    def count_ge(cand):
        def cnt_body(kt, c):
            hit = jnp.where(keys_ref[kt] >= cand, 1, 0)
            return c + jnp.sum(hit.reshape(tk // SUBLANES, SUBLANES, tq), axis=0)
        c = lax.fori_loop(0, nk, cnt_body, jnp.zeros((SUBLANES, tq), I32))
        return jnp.sum(c, axis=0, keepdims=True)

    thr = jnp.maximum(_topk_threshold(count_ge, (1, tq), kk), INT_MIN + 1)

    m_ref[...] = jnp.full(m_ref.shape, NEG, F32)
    l_ref[...] = jnp.zeros(l_ref.shape, F32)
    acc_ref[...] = jnp.zeros(acc_ref.shape, F32)

    def kv_step(kt, bias_of, thr_k):
        mask = keys_ref[kt] >= thr_k
        for h in range(N_HEADS):
            s = _dot(kh_ref[0, h, kt], qT_ref[0, h]) + bias_of(h)
            m_old = m_ref[h]
            m_new = jnp.maximum(m_old, jnp.max(jnp.where(mask, s, NEG), axis=0, keepdims=True))
            p = jnp.where(mask, jnp.exp(s - m_new), 0.0)
            alpha = jnp.exp(m_old - m_new)
            l_ref[h] = alpha * l_ref[h] + jnp.sum(p, axis=0, keepdims=True)
            acc_ref[h] = alpha * acc_ref[h] + _dot(vT_ref[0, h, kt], p.astype(BF))
            m_ref[h] = m_new

    def far_body(kt, carry):
        kv_step(kt, lambda h: cb_ref[h], thr)
        return carry

    lax.fori_loop(0, jnp.maximum(nk - n_toep, 0), far_body, 0)
    for o in reversed(range(n_toep)):
        thr_o = jnp.where(qt >= o, thr, INT_MAX)
        kv_step(jnp.maximum(qt - o, 0), lambda h, o=o: toep_ref[o, h], thr_o)
    for h in range(N_HEADS):
        o_ref[0, h] = acc_ref[h] / l_ref[h]


def _attn_prompt(cb, qT, qiT, wiT, kh, vT, ki, toep, tq, kk):
    b, _, _, s = qT.shape
    nq = s // tq
    n_toep = toep.shape[0]
    per_q = lambda n: pl.BlockSpec((1, N_HEADS, n, tq), lambda i, j: (i, 0, 0, j))
    return pl.pallas_call(
        functools.partial(_attn_prompt_kernel, tq, n_toep, kk),
        grid=(b, nq),
        in_specs=[pl.BlockSpec(memory_space=pltpu.SMEM),
                  per_q(HEAD_DIM), per_q(IDX_DIM),
                  pl.BlockSpec((1, IDX_HEADS, tq), lambda i, j: (i, 0, j)),
                  pl.BlockSpec((1, N_HEADS, nq, tq, HEAD_DIM), lambda i, j: (i, 0, 0, 0, 0)),
                  pl.BlockSpec((1, N_HEADS, nq, HEAD_DIM, tq), lambda i, j: (i, 0, 0, 0, 0)),
                  pl.BlockSpec((1, nq, tq, IDX_DIM), lambda i, j: (i, 0, 0, 0)),
                  _full(toep.shape)],
        out_specs=per_q(HEAD_DIM),
        out_shape=jax.ShapeDtypeStruct((b, N_HEADS, HEAD_DIM, s), F32),
        scratch_shapes=[pltpu.VMEM((nq, tq, tq), I32),
                        pltpu.VMEM((N_HEADS, 1, tq), F32), pltpu.VMEM((N_HEADS, 1, tq), F32),
                        pltpu.VMEM((N_HEADS, HEAD_DIM, tq), F32)],
        compiler_params=_params(2),
        name="attn_prompt",
    )(cb, qT, qiT, wiT, kh, vT, ki, toep)


PAGES_PER_STEP = 8


def _page_specs(block, n_pages):
    def spec(r):
        return pl.BlockSpec(block, lambda b, g, pt: (pt[b * n_pages + g * PAGES_PER_STEP + r],) + (0,) * (len(block) - 1))
    return [spec(r) for r in range(PAGES_PER_STEP)]


def _samp_scores_kernel(pt_ref, qi_ref, wi_ref, *refs):
    pages, o_ref = refs[:PAGES_PER_STEP], refs[PAGES_PER_STEP]
    qi = qi_ref[0]
    w = wi_ref[0] * IDX_HEADS ** -0.5
    rows = []
    for r in range(PAGES_PER_STEP):
        s = _dot_nt(qi, pages[r][0].astype(BF))
        rows.append(jnp.sum(jnp.maximum(s, 0.0) * w, axis=0, keepdims=True))
    o_ref[0, 0] = jnp.concatenate(rows, axis=0)


def _samp_scores(page_table_flat, qi, wi, cache_idx, n_pages):
    db = qi.shape[0]
    ng = n_pages // PAGES_PER_STEP
    grid_spec = pltpu.PrefetchScalarGridSpec(
        num_scalar_prefetch=1,
        grid=(db, ng),
        in_specs=[pl.BlockSpec((1, IDX_HEADS, IDX_DIM), lambda b, g, pt: (b, 0, 0)),
                  pl.BlockSpec((1, IDX_HEADS, 1), lambda b, g, pt: (b, 0, 0))]
                 + _page_specs((1, PAGE_SIZE, IDX_DIM), n_pages),
        out_specs=pl.BlockSpec((1, 1, PAGES_PER_STEP, PAGE_SIZE), lambda b, g, pt: (b, g, 0, 0)),
    )
    return pl.pallas_call(
        _samp_scores_kernel,
        grid_spec=grid_spec,
        out_shape=jax.ShapeDtypeStruct((db, ng, PAGES_PER_STEP, PAGE_SIZE), F32),
        compiler_params=_params(2),
        name="samp_scores",
    )(page_table_flat, qi, wi, *([cache_idx] * PAGES_PER_STEP))


SEQ_PER_STEP = 8


def _total(x):
    return jnp.sum(jnp.sum(x, axis=0, keepdims=True), axis=1, keepdims=True)


def _ind(m):
    return jnp.where(m, 1.0, 0.0)


def _samp_select_kernel(kk, sc_ref, qi_ref, ki8_ref, kw_ref, tri_ref, lt_ref, idx_ref, np_ref, nsel_ref,
                        keys_ref, thr_ref, knew_ref):
    n_pages = sc_ref.shape[1]
    n_slots = idx_ref.shape[2]
    key_new = []
    for r in range(SEQ_PER_STEP):
        prod = qi_ref[r].astype(F32) * ki8_ref[r].astype(F32)
        kwv = kw_ref[r]
        s_new = jnp.zeros((1, 1), F32)
        for h in range(IDX_HEADS):
            s = jnp.sum(prod[:, h * IDX_DIM:(h + 1) * IDX_DIM], axis=1, keepdims=True)
            s_new = s_new + jnp.maximum(s, 0.0) * (kwv[:, IDX_DIM + h:IDX_DIM + h + 1] * IDX_HEADS ** -0.5)
        key_new.append(_monotone_key(s_new))
        keys_ref[r] = _monotone_key(sc_ref[r])

    def bit_body(bi, thr_us):
        bit = jnp.left_shift(jnp.int32(1), 31 - bi)
        out = []
        for r in range(SEQ_PER_STEP):
            cand_u = thr_us[r] | bit
            cand = cand_u ^ INT_MIN
            cnt = _total(jnp.where(keys_ref[r] >= cand, 1, 0)) + jnp.where(key_new[r] >= cand, 1, 0)
            out.append(jnp.where(cnt >= kk, cand_u, thr_us[r]))
        return tuple(out)

    thr_us = lax.fori_loop(0, 32, bit_body, tuple(jnp.zeros((1, 1), I32) for _ in range(SEQ_PER_STEP)))
    for r in range(SEQ_PER_STEP):
        thr_ref[r] = jnp.broadcast_to(thr_us[r] ^ INT_MIN, (1, LANES))
        knew_ref[r] = jnp.broadcast_to(key_new[r], (1, LANES))

    slot_id = lax.broadcasted_iota(I32, (PAGE_SIZE, n_slots), 1).astype(F32)
    in_page = lax.broadcasted_iota(I32, (PAGE_SIZE, n_slots), 0)

    def prefix(x):
        incl = _dot(x.astype(BF), tri_ref[...])
        page_tot = jnp.broadcast_to(incl[:, PAGE_SIZE - 1:PAGE_SIZE], incl.shape)
        return incl - x + _dot(lt_ref[...], page_tot.astype(BF))

    def seq_body(r, carry):
        keys = keys_ref[r]
        thr = thr_ref[r][:, :1]
        kn = knew_ref[r][:, :1]
        gt = keys > thr
        eq = keys == thr
        eq_f = _ind(eq)
        need = kk - (_total(_ind(gt)) + _ind(kn > thr))
        sel = jnp.where(gt, 1.0, jnp.where(eq, _ind(prefix(eq_f) < need), 0.0))
        new_sel = jnp.where(kn > thr, 1.0, jnp.where(kn == thr, _ind(_total(eq_f) < need), 0.0))
        rank = jnp.where(sel > 0.0, prefix(sel), -1.0)
        pad = jnp.full((PAGE_SIZE - n_pages, PAGE_SIZE), -1.0, F32)
        rank_t = jnp.concatenate([rank, pad], axis=0).T
        acc = jnp.zeros((PAGE_SIZE, n_slots), I32)
        for p in range(n_pages):
            acc = acc + jnp.where(rank_t[:, p:p + 1] == slot_id, in_page + p * PAGE_SIZE, 0)
        idx_ref[r] = jnp.sum(acc, axis=0, keepdims=True)
        np_ref[r] = jnp.broadcast_to(_total(sel), (1, LANES))
        nsel_ref[r] = jnp.broadcast_to(new_sel, (1, LANES))
        return carry

    lax.fori_loop(0, SEQ_PER_STEP, seq_body, 0)


def _samp_select(scores, qib, ki8, kw, kk):
    db, n_pages, _ = scores.shape
    assert n_pages <= PAGE_SIZE, "index compaction transposes one (pages, positions) tile"
    spb = SEQ_PER_STEP if db % SEQ_PER_STEP == 0 else db
    assert spb == SEQ_PER_STEP
    pos = np.arange(PAGE_SIZE)
    tri = jnp.asarray(pos[:, None] <= pos[None, :], BF)
    pg = np.arange(n_pages)
    lt = jnp.asarray(pg[None, :] < pg[:, None], BF)
    per_seq = lambda n: pl.BlockSpec((spb, 1, n), lambda i: (i, 0, 0))
    return pl.pallas_call(
        functools.partial(_samp_select_kernel, kk),
        grid=(db // spb,),
        in_specs=[pl.BlockSpec((spb, n_pages, PAGE_SIZE), lambda i: (i, 0, 0)),
                  per_seq(qib.shape[2]), per_seq(ki8.shape[2]), per_seq(kw.shape[2]),
                  _full(tri.shape), _full(lt.shape)],
        out_specs=[per_seq(kk), per_seq(LANES), per_seq(LANES)],
        out_shape=[jax.ShapeDtypeStruct((db, 1, kk), I32),
                   jax.ShapeDtypeStruct((db, 1, LANES), F32),
                   jax.ShapeDtypeStruct((db, 1, LANES), F32)],
        scratch_shapes=[pltpu.VMEM((spb, n_pages, PAGE_SIZE), I32),
                        pltpu.VMEM((spb, 1, LANES), I32), pltpu.VMEM((spb, 1, LANES), I32)],
        compiler_params=_params(1),
        name="samp_select",
    )(scores, qib, ki8, kw, tri, lt)


def _samp_attn_kernel(n_pages, idx_ref, pt_ref, q_ref, bias_ref, bnew_ref, kn_ref, vn_ref, ck_ref, cv_ref,
                      o_ref, kbuf, vbuf, sem):
    b = pl.program_id(0)
    n_slots = kbuf.shape[1] // N_HEADS

    def row_copies(seq, s, buf):
        idx = idx_ref[seq * n_slots + s]
        phys = pt_ref[seq * n_pages + idx // PAGE_SIZE]
        off = idx % PAGE_SIZE
        dst = pl.ds(pl.multiple_of(s * N_HEADS, N_HEADS), N_HEADS)
        return (pltpu.make_async_copy(ck_ref.at[0, phys, off], kbuf.at[buf, dst, :], sem.at[buf, 0]),
                pltpu.make_async_copy(cv_ref.at[0, phys, off], vbuf.at[buf, dst, :], sem.at[buf, 1]))

    def for_rows(seq, buf, action):
        def body(s, carry):
            for cp in row_copies(seq, s, buf):
                action(cp)
            return carry
        lax.fori_loop(0, n_slots, body, 0, unroll=4)

    @pl.when(b == 0)
    def _():
        for_rows(0, 0, lambda cp: cp.start())

    @pl.when(b + 1 < pl.num_programs(0))
    def _():
        for_rows(b + 1, (b + 1) % 2, lambda cp: cp.start())

    buf = b % 2
    for_rows(b, buf, lambda cp: cp.wait())

    q = q_ref[0]
    k3 = kbuf[buf].reshape(n_slots, N_HEADS, HEAD_DIM)
    v3 = vbuf[buf].reshape(n_slots, N_HEADS, HEAD_DIM)
    logit = jnp.sum(k3 * q[None], axis=2, keepdims=True) + bias_ref[0].reshape(n_slots, N_HEADS, 1)
    s_new = jnp.sum(kn_ref[0] * q, axis=1, keepdims=True) + bnew_ref[0]
    m = jnp.maximum(jnp.max(logit, axis=0), s_new)
    p = jnp.exp(logit - m[None])
    p_new = jnp.exp(s_new - m)
    l = jnp.sum(p, axis=0) + p_new
    acc = jnp.sum(p * v3, axis=0) + p_new * vn_ref[0]
    o_ref[0] = acc / l


def _samp_attn(idx_flat, page_table_flat, q, bias, bias_new, kn, vn, cache_k, cache_v, n_slots, n_pages):
    db = q.shape[0]
    per_seq = lambda shape: pl.BlockSpec((1,) + shape, lambda b, idx, pt: (b,) + (0,) * len(shape))
    head_tile = per_seq((N_HEADS, HEAD_DIM))
    grid_spec = pltpu.PrefetchScalarGridSpec(
        num_scalar_prefetch=2,
        grid=(db,),
        in_specs=[head_tile, per_seq((n_slots * N_HEADS, 1)), per_seq((N_HEADS, 1)), head_tile, head_tile,
                  pl.BlockSpec(memory_space=pl.ANY), pl.BlockSpec(memory_space=pl.ANY)],
        out_specs=head_tile,
        scratch_shapes=[pltpu.VMEM((2, n_slots * N_HEADS, HEAD_DIM), F32),
                        pltpu.VMEM((2, n_slots * N_HEADS, HEAD_DIM), F32),
                        pltpu.SemaphoreType.DMA((2, 2))],
    )
    return pl.pallas_call(
        functools.partial(_samp_attn_kernel, n_pages),
        grid_spec=grid_spec,
        out_shape=jax.ShapeDtypeStruct((db, N_HEADS, HEAD_DIM), F32),
        compiler_params=_params(1),
        name="samp_attn",
    )(idx_flat, page_table_flat, q, bias, bias_new, kn, vn, cache_k, cache_v)


def _merge_kernel(per_row, x_ref, yc_ref, at_ref, sgc_ref, sga_ref, mod_ref, lng_ref, lnb_ref,
                  wco_ref, wao_ref, wout_ref, g2_ref, wpq_ref, kbd_ref,
                  xm_ref, h2t_ref, st_ref):
    yc = yc_ref[...]
    mu = jnp.mean(yc, axis=-1, keepdims=True)
    yd = yc - mu
    ln = yd * lax.rsqrt(jnp.mean(yd * yd, axis=-1, keepdims=True) + EPS) * lng_ref[...] + lnb_ref[...]
    act = ln * jax.nn.sigmoid(ln)
    conv_out = _dot(act.astype(BF), wco_ref[...])
    attn_out = _dot(at_ref[0].T.astype(BF), wao_ref[...])
    merged = sgc_ref[...] * conv_out + sga_ref[...] * attn_out
    xm = x_ref[...] + _mod(mod_ref, 2, per_row) * _dot(merged.astype(BF), wout_ref[...])
    xm_ref[...] = xm
    rs = lax.rsqrt(jnp.mean(xm * xm, axis=-1, keepdims=True) + EPS)
    h2 = (xm * rs) * g2_ref[...] * (1.0 + _mod(mod_ref, 4, per_row)) + _mod(mod_ref, 3, per_row)
    h2t_ref[...] = h2.T.astype(BF)
    pq = _dot(h2.astype(BF), wpq_ref[...])
    st_ref[...] = _dot_nt(kbd_ref[...], pq.astype(BF))


def _merge(x, yc, at, sgc, sga, mod, per_row, rows_per_seq, tm, wts):
    t = x.shape[0]
    row = lambda n: pl.BlockSpec((tm, n), lambda i: (i, 0))
    col = lambda n: pl.BlockSpec((n, tm), lambda i: (0, i))
    n_scores = PEER_HEADS * 2 * PEER_KEYS
    return pl.pallas_call(
        functools.partial(_merge_kernel, per_row),
        grid=(t // tm,),
        in_specs=[row(D_MODEL), row(CONV_DIM),
                  pl.BlockSpec((1, ATTN_DIM, tm), lambda i: (i // (at.shape[2] // tm), 0, i % (at.shape[2] // tm))),
                  row(D_MODEL), row(D_MODEL),
                  _mod_spec(per_row, tm, rows_per_seq)] + [_full(w.shape) for w in wts],
        out_specs=[row(D_MODEL), col(D_MODEL), col(n_scores)],
        out_shape=[jax.ShapeDtypeStruct((t, D_MODEL), F32),
                   jax.ShapeDtypeStruct((D_MODEL, t), BF),
                   jax.ShapeDtypeStruct((n_scores, t), F32)],
        compiler_params=_params(1),
        name="merge",
    )(x, yc, at, sgc, sga, mod, *wts)


N_EXTRACT = PEER_TOPK + 1


def _top_extract(cur, n):
    outs = []
    for _ in range(n):
        mx = jnp.max(cur, axis=0, keepdims=True)
        outs.append(mx)
        cur = jnp.where(cur >= mx, -jnp.inf, cur)
    return outs


RANK_NONE = float(PEER_KEYS - 1)


def _peer_select_kernel(lb, st_ref, r1_ref, e1_ref, n_ref, al_ref):
    tl = st_ref.shape[1]

    def head_body(h, carry):
        base = pl.multiple_of(h * 2 * PEER_KEYS, 2 * PEER_KEYS)
        s0 = st_ref[pl.ds(base, PEER_KEYS), :]
        s1 = st_ref[pl.ds(base + PEER_KEYS, PEER_KEYS), :]
        a0 = _top_extract(s0, N_EXTRACT)
        a1 = []
        cur = s1
        rank1 = jnp.full(s1.shape, RANK_NONE, F32)
        for k in range(N_EXTRACT):
            mx = jnp.max(cur, axis=0, keepdims=True)
            a1.append(mx)
            hit = cur >= mx
            if k < PEER_TOPK:
                rank1 = jnp.where(hit, float(k), rank1)
            cur = jnp.where(hit, -jnp.inf, cur)
        pad = [jnp.full((1, tl), -jnp.inf, F32)] * (3 * SUBLANES - N_EXTRACT)
        a1_all = jnp.concatenate(a1 + pad, axis=0)
        cands = [a0[0] + a1_all] + [a0[k] + a1_all[:SUBLANES] for k in range(1, N_EXTRACT)]
        top = _top_extract(jnp.concatenate(cands, axis=0), N_EXTRACT)
        thr = 0.5 * (top[PEER_TOPK - 1] + top[PEER_TOPK])
        z = jnp.zeros_like(thr)
        for k in range(PEER_TOPK):
            z = z + jnp.exp(top[k] - top[0])
        tau = thr - s0
        cnt = jnp.zeros(s0.shape, F32)
        for l in range(PEER_TOPK):
            cnt = cnt + jnp.where(a1[l] >= tau, 1.0, 0.0)
        e1 = jnp.exp(s1 - a1[0])
        al = jnp.exp(s0 - a0[0]) / z
        for tb in range(tl // lb):
            ts = slice(tb * lb, (tb + 1) * lb)
            pack = lambda v: pltpu.bitcast(v[:, ts].astype(BF), I32)
            r1_ref[h, tb] = pack(rank1)
            e1_ref[h, tb] = pack(e1)
            n_ref[h, tb] = pack(cnt)
            al_ref[h, tb] = pack(al)
        return carry

    lax.fori_loop(0, PEER_HEADS, head_body, 0)


def _peer_select(st, tl):
    t = st.shape[1]
    lb = min(LANES, t)
    spec = pl.BlockSpec((PEER_HEADS, tl // lb, PEER_KEYS // 2, lb), lambda i: (0, i, 0, 0))
    shp = jax.ShapeDtypeStruct((PEER_HEADS, t // lb, PEER_KEYS // 2, lb), I32)
    return pl.pallas_call(
        functools.partial(_peer_select_kernel, lb),
        grid=(t // tl,),
        in_specs=[pl.BlockSpec((st.shape[0], tl), lambda i: (0, i))],
        out_specs=[spec] * 4,
        out_shape=[shp] * 4,
        compiler_params=_params(1),
        name="peer_select",
    )(st)


PEER_ROWS_PER_STEP = 2 * SUBLANES


def _peer_main_kernel(per_row, h2t_ref, u_ref, vt_ref, r1_ref, e1_ref, n_ref, al_ref, xm_ref, mod_ref,
                      o_ref, acc_ref, a_ref, g_ref):
    j = pl.program_id(1)
    tm = h2t_ref.shape[1]
    n_tb, _, lb = r1_ref.shape[1:]

    @pl.when(j == 0)
    def _():
        acc_ref[...] = jnp.zeros(acc_ref.shape, F32)

    i0 = pl.multiple_of(j * SUBLANES, SUBLANES)
    h2t = h2t_ref[...]
    unit = 2 * PEER_KEYS
    n_units = PEER_ROWS_PER_STEP // 2

    def activations(un):
        a_ref[un % 2] = _dot(u_ref[un * unit:(un + 1) * unit, :], h2t)

    activations(0)
    for un in range(n_units):
        rows = slice(un * unit, (un + 1) * unit)
        if un + 1 < n_units:
            activations(un + 1)
        for tb in range(n_tb):
            ts = slice(tb * lb, (tb + 1) * lb)
            w = [jnp.zeros((PEER_KEYS, lb), BF), jnp.zeros((PEER_KEYS, lb), BF)]
            for h in range(PEER_HEADS):
                cnt = pltpu.bitcast(n_ref[h, tb, pl.ds(i0, SUBLANES), :], BF)
                al = pltpu.bitcast(al_ref[h, tb, pl.ds(i0, SUBLANES), :], BF)
                r1 = pltpu.bitcast(r1_ref[h, tb], BF)
                e1 = pltpu.bitcast(e1_ref[h, tb], BF)
                for r in range(2):
                    ii = 2 * un + r
                    cnt_b = jnp.broadcast_to(cnt[ii:ii + 1], (PEER_KEYS, lb))
                    al_b = jnp.broadcast_to(al[ii:ii + 1], (PEER_KEYS, lb))
                    w[r] = w[r] + jnp.where(r1 < cnt_b, e1 * al_b, 0)
            for r in range(2):
                a = a_ref[un % 2, r * PEER_KEYS:(r + 1) * PEER_KEYS, ts]
                gelu = 0.5 * a * (1.0 + lax.erf(a * (2.0 ** -0.5)))
                g_ref[un * unit + r * PEER_KEYS:un * unit + (r + 1) * PEER_KEYS, ts] = w[r] * gelu.astype(BF)
        acc_ref[...] += _dot(vt_ref[:, rows], g_ref[rows, :])

    @pl.when(j == pl.num_programs(1) - 1)
    def _():
        o_ref[...] = xm_ref[...] + _mod(mod_ref, 5, per_row) * acc_ref[...].T


def _peer_main(h2t, u_bf, vt_bf, r1, e1, cnt, al, xm, mod, per_row, rows_per_seq, tm):
    t = xm.shape[0]
    ec = PEER_ROWS_PER_STEP * PEER_KEYS
    n_e = u_bf.shape[0]
    lb = r1.shape[3]
    sel = pl.BlockSpec((PEER_HEADS, tm // lb, PEER_KEYS // 2, lb), lambda i, j: (0, i, 0, 0))
    if per_row:
        mod_spec = pl.BlockSpec((6, tm, D_MODEL), lambda i, j: (0, i, 0))
    else:
        tiles_per_seq = rows_per_seq // tm
        mod_spec = pl.BlockSpec((1, 6, D_MODEL), lambda i, j: (i // tiles_per_seq, 0, 0))
    return pl.pallas_call(
        functools.partial(_peer_main_kernel, per_row),
        grid=(t // tm, n_e // ec),
        in_specs=[pl.BlockSpec((D_MODEL, tm), lambda i, j: (0, i)),
                  pl.BlockSpec((ec, D_MODEL), lambda i, j: (j, 0)),
                  pl.BlockSpec((D_MODEL, ec), lambda i, j: (0, j)),
                  sel, sel, sel, sel,
                  pl.BlockSpec((tm, D_MODEL), lambda i, j: (i, 0)),
                  mod_spec],
        out_specs=pl.BlockSpec((tm, D_MODEL), lambda i, j: (i, 0)),
        out_shape=jax.ShapeDtypeStruct((t, D_MODEL), F32),
        scratch_shapes=[pltpu.VMEM((D_MODEL, tm), F32), pltpu.VMEM((2, 2 * PEER_KEYS, tm), F32),
                        pltpu.VMEM((ec, tm), BF)],
        compiler_params=_params(2),
        name="peer_main",
    )(h2t, u_bf, vt_bf, r1, e1, cnt, al, xm, mod)


def _head_major(a, b, s):
    return a.reshape(b, s, N_HEADS, -1).transpose(0, 2, 1, 3)


def _tile(n, pref):
    return pref if n % pref == 0 else n


def kernel(x_prompt, x_sample, cache_k, cache_v, cache_idx_k, state_conv, page_table, c_prompt, c_sample,
           w_ada, b_ada, norm1_g, w_in, q_norm_g, k_norm_g, rel_bias, w_dw, b_dw, conv_ln_g, conv_ln_b,
           w_conv_o, w_attn_o, w_out, norm2_g, w_peer_q, peer_keys, peer_u, peer_v):
    b, s, _ = x_prompt.shape
    db, ds, _ = x_sample.shape
    n_pages = page_table.shape[1]
    past = n_pages * PAGE_SIZE
    assert w_ada.shape[0] == 1, "one layer"
    assert ds == 1, "one new token per sample sequence"
    assert n_pages % PAGES_PER_STEP == 0
    tp = b * s

    w = w_in[0]
    pts = np.cumsum([0, 2 * CONV_DIM, ATTN_DIM, ATTN_DIM, ATTN_DIM, IDX_HEADS * IDX_DIM, IDX_DIM, IDX_HEADS,
                     D_MODEL, D_MODEL])
    seg = lambda i: w[:, pts[i]:pts[i + 1]].astype(BF)
    w_kw = jnp.pad(w[:, pts[5]:pts[7]], ((0, 0), (0, LANES - IDX_DIM - IDX_HEADS))).astype(BF)
    head_of = np.arange(ATTN_DIM) // HEAD_DIM
    bd = jnp.asarray(head_of[:, None] == head_of[None, :], BF)
    row = lambda v: v.reshape(1, -1)
    in_wts = [row(norm1_g[0]), seg(0), seg(1), seg(2), seg(3), seg(4), w_kw, seg(7), seg(8),
              row(jnp.tile(q_norm_g[0], N_HEADS)), row(jnp.tile(k_norm_g[0], N_HEADS)), bd]
    n_half = PEER_HEADS * 2
    kb16 = peer_keys[0].reshape(n_half, PEER_KEYS, PEER_HALF)
    kbd_t = (jnp.eye(n_half, dtype=F32)[:, None, :, None] * kb16[:, :, None, :]).reshape(
        n_half * PEER_KEYS, n_half * PEER_HALF).astype(BF)
    merge_wts = [row(conv_ln_g[0]), row(conv_ln_b[0]), w_conv_o[0].astype(BF), w_attn_o[0].astype(BF),
                 w_out[0].astype(BF), row(norm2_g[0]), w_peer_q[0].astype(BF), kbd_t]
    u_bf = peer_u[0].astype(BF)
    vt_bf = peer_v[0].T.astype(BF)

    mod = _ada(jnp.concatenate([c_prompt, c_sample], axis=0), w_ada[0].astype(BF), row(b_ada[0]))
    mod_p = mod[:b].reshape(b, 6, D_MODEL)
    mod_s = mod[b:].reshape(db, 6, D_MODEL).transpose(1, 0, 2)

    tm_p = _tile(s, 256)
    glu_p, k_p, v_p, kw_p, qb_p, kb_p, vb_p, qib_p, sgc_p, sga_p = _inproj(
        x_prompt.reshape(tp, D_MODEL), mod_p, False, s, tm_p, in_wts)
    glu_s, k_s, v_s, kw_s, qb_s, kb_s, vb_s, qib_s, sgc_s, sga_s = _inproj(
        x_sample.reshape(db, D_MODEL), mod_s, True, 1, db, in_wts)

    yc_p = _conv_prompt(glu_p.reshape(b, s, CONV_DIM), w_dw[0], row(b_dw[0]), _tile(s, 256)).reshape(tp, CONV_DIM)
    yc_s = _conv_sample(state_conv[0].transpose(1, 0, 2), glu_s, w_dw[0], row(b_dw[0]))

    tq = _tile(s, 256)
    nq = s // tq
    bucket = _rel_bucket_np(np.arange(max(s, past + 1)))
    n_toep = nq
    while n_toep > 0 and len(set(bucket[max((n_toep - 1) * tq - (tq - 1), 0):n_toep * tq])) == 1 \
            and bucket[(n_toep - 1) * tq] == bucket[s - 1]:
        n_toep -= 1
    n_toep = max(n_toep, 1)
    ii = np.arange(tq)
    dist = np.arange(n_toep)[:, None, None] * tq + ii[None, None, :] - ii[None, :, None]
    toep = rel_bias[_rel_bucket_np(dist)].transpose(0, 3, 1, 2)
    cb = rel_bias[int(bucket[s - 1])]

    kk_p = min(IDX_TOPK_MAX, s // 4)
    q_minor = lambda a: a.reshape(b, s, N_HEADS, -1).transpose(0, 2, 3, 1)
    kh = kb_p.reshape(b, nq, tq, N_HEADS, HEAD_DIM).transpose(0, 3, 1, 2, 4)
    vT = vb_p.reshape(b, nq, tq, N_HEADS, HEAD_DIM).transpose(0, 3, 1, 4, 2)
    ki = kw_p[:, :IDX_DIM].astype(BF).reshape(b, nq, tq, IDX_DIM)
    wiT = kw_p[:, IDX_DIM:IDX_DIM + IDX_HEADS].reshape(b, s, IDX_HEADS).transpose(0, 2, 1)
    at_p = _attn_prompt(cb, q_minor(qb_p), q_minor(qib_p), wiT, kh, vT, ki, toep, tq, kk_p)
    at_p = at_p.reshape(b, ATTN_DIM, s)

    kk_s = min(IDX_TOPK_MAX, (past + ds) // 4)
    pt_flat = page_table.reshape(-1)
    sc = _samp_scores(pt_flat, qib_s.reshape(db, IDX_HEADS, IDX_DIM),
                      kw_s[:, IDX_DIM:IDX_DIM + IDX_HEADS].reshape(db, IDX_HEADS, 1),
                      cache_idx_k[0], n_pages)
    ki8 = jnp.tile(kw_s[:, :IDX_DIM].astype(BF), (1, IDX_HEADS))
    idx_s, n_past, new_sel = _samp_select(sc.reshape(db, n_pages, PAGE_SIZE), qib_s[:, None, :], ki8[:, None, :],
                                          kw_s[:, None, :], kk_s)
    idx_s = idx_s.reshape(db, kk_s)
    bias_past = rel_bias[_rel_bucket_np(past - np.arange(past))]
    slot_used = jnp.arange(kk_s, dtype=F32)[None, :] < n_past[:, 0, :1]
    bias_s = jnp.where(slot_used[:, :, None], jnp.take(bias_past, idx_s, axis=0), NEG)
    bias_new = jnp.where(new_sel[:, 0, :1] > 0.0, rel_bias[int(_rel_bucket_np(np.zeros((), np.int64)))][None], NEG)
    heads = lambda a: a.reshape(db, N_HEADS, HEAD_DIM)
    at_s = _samp_attn(idx_s.reshape(-1), pt_flat, heads(qb_s.astype(F32)), bias_s.reshape(db, kk_s * N_HEADS, 1),
                      bias_new.reshape(db, N_HEADS, 1), heads(k_s), heads(v_s), cache_k, cache_v,
                      kk_s, n_pages).reshape(db, ATTN_DIM)

    def tail(x, yc, at, sgc, sga, mod_g, per_row, rows_per_seq, tm_merge, tl_sel, tm_peer):
        xm, h2t, st = _merge(x, yc, at, sgc, sga, mod_g, per_row, rows_per_seq, tm_merge, merge_wts)
        s1, e1, tau, al = _peer_select(st, tl_sel)
        return _peer_main(h2t, u_bf, vt_bf, s1, e1, tau, al, xm, mod_g, per_row, rows_per_seq, tm_peer)

    y_p = tail(x_prompt.reshape(tp, D_MODEL), yc_p, at_p, sgc_p, sga_p, mod_p, False, s,
               _tile(s, 256), _tile(tp, 256), _tile(s, 512))
    y_s = tail(x_sample.reshape(db, D_MODEL), yc_s, at_s.T[None], sgc_s, sga_s, mod_s, True, 1, db, db, db)

    glu_p3 = glu_p.reshape(b, s, CONV_DIM)
    conv_s = jnp.concatenate([state_conv[0], glu_s[:, None, :]], axis=1)[:, -CONV_STATE:]
    return (y_p.reshape(b, s, D_MODEL), y_s.reshape(db, ds, D_MODEL),
            k_p.reshape(1, b, s, N_HEADS, HEAD_DIM), v_p.reshape(1, b, s, N_HEADS, HEAD_DIM),
            kw_p[:, :IDX_DIM].reshape(1, b, s, IDX_DIM), glu_p3[None, :, -CONV_STATE:],
            k_s.reshape(1, db, ds, N_HEADS, HEAD_DIM), v_s.reshape(1, db, ds, N_HEADS, HEAD_DIM),
            kw_s[:, :IDX_DIM].reshape(1, db, ds, IDX_DIM), conv_s[None])
```

```python
import functools
import math

import numpy as np
import jax
import jax.numpy as jnp
from jax import lax
from jax.experimental import pallas as pl
from jax.experimental.pallas import tpu as pltpu

F32 = jnp.float32
BF = jnp.bfloat16
I32 = jnp.int32

D_MODEL = 1024
N_HEADS = 8
HEAD_DIM = 64
ATTN_DIM = N_HEADS * HEAD_DIM
IDX_HEADS = 8
IDX_DIM = 64
IDX_TOPK_MAX = 256
REL_BUCKETS = 32
REL_MAX_DIST = 128
CONV_DIM = 512
CONV_WIDTH = 31
CONV_STATE = CONV_WIDTH - 1
PEER_HEADS = 8
PEER_KEYS = 128
PEER_HALF = 64
PEER_TOPK = 16
PAGE_SIZE = 128
EPS = 1e-6

NEG = -1e30
INT_MIN = -(2 ** 31)
INT_MAX = 2 ** 31 - 1
LANES = 128
SUBLANES = 8
VMEM_LIMIT = 52 * 1024 * 1024


def _params(n_axes):
    return pltpu.CompilerParams(dimension_semantics=("arbitrary",) * n_axes,
                                vmem_limit_bytes=VMEM_LIMIT)


def _full(shape):
    zeros = (0,) * len(shape)
    return pl.BlockSpec(shape, lambda *_: zeros)


def _mod(mod_ref, k, per_row):
    return mod_ref[k] if per_row else mod_ref[0, k:k + 1, :]


def _mod_spec(per_row, tm, rows_per_seq):
    if per_row:
        return pl.BlockSpec((6, tm, D_MODEL), lambda i: (0, i, 0))
    tiles_per_seq = rows_per_seq // tm
    return pl.BlockSpec((1, 6, D_MODEL), lambda i: (i // tiles_per_seq, 0, 0))


def _dot(a, b):
    return jnp.dot(a, b, preferred_element_type=F32)


def _dot_nt(a, b):
    return lax.dot_general(a, b, (((1,), (1,)), ((), ())), preferred_element_type=F32)


def _monotone_key(x):
    bits = lax.bitcast_convert_type(x, I32)
    return bits ^ ((bits >> 31) & INT_MAX)


def _ada_kernel(c_ref, w_ref, b_ref, o_ref):
    c = c_ref[...]
    a = (c * jax.nn.sigmoid(c)).astype(BF)
    o_ref[...] = _dot(a, w_ref[...]) + b_ref[...]


def _ada(c, w_bf, b):
    rows = c.shape[0]
    n = w_bf.shape[1]
    nb = 1536
    return pl.pallas_call(
        _ada_kernel,
        grid=(n // nb,),
        in_specs=[_full((rows, D_MODEL)),
                  pl.BlockSpec((D_MODEL, nb), lambda j: (0, j)),
                  pl.BlockSpec((1, nb), lambda j: (0, j))],
        out_specs=pl.BlockSpec((rows, nb), lambda j: (0, j)),
        out_shape=jax.ShapeDtypeStruct((rows, n), F32),
        compiler_params=_params(1),
        name="ada",
    )(c, w_bf, b)


def _inproj_kernel(per_row, x_ref, mod_ref, g1_ref, wglu_ref, wq_ref, wk_ref, wv_ref, wqi_ref,
                   wkw_ref, wgc_ref, wga_ref, gq_ref, gk_ref, bd_ref,
                   glu_ref, sgc_ref, sga_ref, *attn_refs):
    x = x_ref[...]
    shift = _mod(mod_ref, 0, per_row)
    scale = _mod(mod_ref, 1, per_row)
    rs = lax.rsqrt(jnp.mean(x * x, axis=-1, keepdims=True) + EPS)
    h = (x * rs) * g1_ref[...] * (1.0 + scale) + shift
    hb = h.astype(BF)

    def head_norm(z, g_ref):
        sq = z * z
        hi = sq.astype(BF)
        lo = (sq - hi.astype(F32)).astype(BF)
        ss = _dot(hi, bd_ref[...]) + _dot(lo, bd_ref[...])
        return z * lax.rsqrt(ss * (1.0 / HEAD_DIM) + EPS) * g_ref[...]

    gl = _dot(hb, wglu_ref[...])
    glu_ref[...] = gl[:, :CONV_DIM] * jax.nn.sigmoid(gl[:, CONV_DIM:])
    q = head_norm(_dot(hb, wq_ref[...]), gq_ref)
    k = head_norm(_dot(hb, wk_ref[...]), gk_ref)
    v = _dot(hb, wv_ref[...])
    qs = q * HEAD_DIM ** -0.5
    qis = _dot(hb, wqi_ref[...]) * IDX_DIM ** -0.5
    kw = _dot(hb, wkw_ref[...])
    sgc_ref[...] = jax.nn.sigmoid(_dot(hb, wgc_ref[...]))
    sga_ref[...] = jax.nn.sigmoid(_dot(hb, wga_ref[...]))
    if per_row:
        k_ref, v_ref, kw_ref, qb_ref, qib_ref = attn_refs
        k_ref[...] = k
        v_ref[...] = v
        kw_ref[...] = kw
        qb_ref[...] = qs.astype(BF)
        qib_ref[...] = qis.astype(BF)
    else:
        kT_ref, vT_ref, kiT_ref, qTb_ref, qiTb_ref, vTb_ref, kh_ref, ki_ref, wiT_ref = attn_refs
        v_t = v.T
        kw_t = kw.T
        kT_ref[0] = k.T
        vT_ref[0] = v_t
        kiT_ref[0] = kw_t[:IDX_DIM, :]
        qTb_ref[0, 0] = qs.T.astype(BF)
        qiTb_ref[0, 0] = qis.T.astype(BF)
        vTb_ref[0, 0] = v_t.astype(BF)
        for hd in range(N_HEADS):
            kh_ref[0, hd, 0] = k[:, hd * HEAD_DIM:(hd + 1) * HEAD_DIM].astype(BF)
        ki_ref[0, 0] = kw[:, :IDX_DIM].astype(BF)
        wiT_ref[0] = kw_t[IDX_DIM:IDX_DIM + IDX_HEADS, :]


def _inproj(x, mod, per_row, rows_per_seq, tm, wts):
    t = x.shape[0]
    row = lambda n: pl.BlockSpec((tm, n), lambda i: (i, 0))
    rows = lambda n, dt: jax.ShapeDtypeStruct((t, n), dt)
    out_specs = [row(CONV_DIM), row(D_MODEL), row(D_MODEL)]
    out_shape = [rows(CONV_DIM, F32), rows(D_MODEL, F32), rows(D_MODEL, F32)]
    if per_row:
        out_specs += [row(ATTN_DIM), row(ATTN_DIM), row(LANES), row(ATTN_DIM), row(ATTN_DIM)]
        out_shape += [rows(ATTN_DIM, F32), rows(ATTN_DIM, F32), rows(LANES, F32), rows(ATTN_DIM, BF), rows(ATTN_DIM, BF)]
    else:
        nq = rows_per_seq // tm
        b = t // rows_per_seq
        seq_minor = lambda n: pl.BlockSpec((1, n, tm), lambda i: (i // nq, 0, i % nq))
        minor = pl.BlockSpec((1, 1, ATTN_DIM, tm), lambda i: (i // nq, i % nq, 0, 0))
        minor_shape = jax.ShapeDtypeStruct((b, nq, ATTN_DIM, tm), BF)
        out_specs += [seq_minor(ATTN_DIM), seq_minor(ATTN_DIM), seq_minor(IDX_DIM),
                      minor, minor, minor,
                      pl.BlockSpec((1, N_HEADS, 1, tm, HEAD_DIM), lambda i: (i // nq, 0, i % nq, 0, 0)),
                      pl.BlockSpec((1, 1, tm, IDX_DIM), lambda i: (i // nq, i % nq, 0, 0)),
                      seq_minor(IDX_HEADS)]
        out_shape += [jax.ShapeDtypeStruct((b, ATTN_DIM, rows_per_seq), F32),
                      jax.ShapeDtypeStruct((b, ATTN_DIM, rows_per_seq), F32),
                      jax.ShapeDtypeStruct((b, IDX_DIM, rows_per_seq), F32),
                      minor_shape, minor_shape, minor_shape,
                      jax.ShapeDtypeStruct((b, N_HEADS, nq, tm, HEAD_DIM), BF),
                      jax.ShapeDtypeStruct((b, nq, tm, IDX_DIM), BF),
                      jax.ShapeDtypeStruct((b, IDX_HEADS, rows_per_seq), F32)]
    return pl.pallas_call(
        functools.partial(_inproj_kernel, per_row),
        grid=(t // tm,),
        in_specs=[row(D_MODEL), _mod_spec(per_row, tm, rows_per_seq)] + [_full(w.shape) for w in wts],
        out_specs=out_specs,
        out_shape=out_shape,
        compiler_params=_params(1),
        name="inproj",
    )(x, mod, *wts)


CONV_HALO = 32


def _conv_prompt_kernel(tt, glu_ref, wdw_ref, bdw_ref, y_ref, ext_ref):
    j = pl.program_id(1)

    @pl.when(j == 0)
    def _():
        ext_ref[0:CONV_HALO, :] = jnp.zeros((CONV_HALO, CONV_DIM), F32)
        ext_ref[CONV_HALO:, :] = glu_ref[0]

    n = tt + CONV_HALO
    t0 = pl.multiple_of(j * tt, tt)
    win = ext_ref[pl.ds(t0, n), :]
    acc = jnp.zeros((tt, CONV_DIM), F32) + bdw_ref[...]
    lead = CONV_HALO - CONV_STATE
    for r in range(SUBLANES):
        rolled = win if r == 0 else pltpu.roll(win, n - r, axis=0)
        for tap in range(CONV_WIDTH):
            off = lead + tap
            if off % SUBLANES == r:
                a = off - r
                acc = acc + wdw_ref[tap:tap + 1, :] * rolled[a:a + tt]
    y_ref[0] = acc


def _conv_prompt(glu, wdw, bdw, tt):
    b, s, _ = glu.shape
    return pl.pallas_call(
        functools.partial(_conv_prompt_kernel, tt),
        grid=(b, s // tt),
        in_specs=[pl.BlockSpec((1, s, CONV_DIM), lambda i, j: (i, 0, 0)),
                  _full(wdw.shape), _full(bdw.shape)],
        out_specs=pl.BlockSpec((1, tt, CONV_DIM), lambda i, j: (i, j, 0)),
        out_shape=jax.ShapeDtypeStruct((b, s, CONV_DIM), F32),
        scratch_shapes=[pltpu.VMEM((s + CONV_HALO, CONV_DIM), F32)],
        compiler_params=_params(2),
        name="conv_prompt",
    )(glu, wdw, bdw)


def _conv_sample_kernel(st_ref, glu_ref, wdw_ref, bdw_ref, y_ref):
    acc = bdw_ref[...] + wdw_ref[CONV_STATE:CONV_STATE + 1, :] * glu_ref[...]
    for j in range(CONV_STATE):
        acc = acc + wdw_ref[j:j + 1, :] * st_ref[j]
    y_ref[...] = acc


def _conv_sample(state_t, glu, wdw, bdw):
    db = glu.shape[0]
    return pl.pallas_call(
        _conv_sample_kernel,
        grid=(1,),
        in_specs=[_full(state_t.shape), _full(glu.shape), _full(wdw.shape), _full(bdw.shape)],
        out_specs=_full((db, CONV_DIM)),
        out_shape=jax.ShapeDtypeStruct((db, CONV_DIM), F32),
        compiler_params=_params(1),
        name="conv_sample",
    )(state_t, glu, wdw, bdw)


def _rel_bucket_np(dist):
    max_exact = REL_BUCKETS // 2
    d = np.maximum(dist, 0)
    df = np.maximum(d, 1).astype(np.float32)
    large = max_exact + (np.log(df / max_exact) / math.log(REL_MAX_DIST / max_exact)
                         * (REL_BUCKETS - max_exact)).astype(np.int32)
    return np.where(d < max_exact, d, np.minimum(large, REL_BUCKETS - 1)).astype(np.int32)


def _topk_threshold(count_ge, shape, kk):
    def bit_body(bi, thr_u):
        cand_u = thr_u | jnp.left_shift(jnp.int32(1), 31 - bi)
        cnt = count_ge(cand_u ^ INT_MIN)
        return jnp.where(cnt >= kk, cand_u, thr_u)

    thr_u = lax.fori_loop(0, 32, bit_body, jnp.zeros(shape, I32))
    return thr_u ^ INT_MIN


def _attn_prompt_kernel(tq, n_toep, kk, cb_ref, qT_ref, qiT_ref, wiT_ref, kh_ref, vT_ref, ki_ref, toep_ref,
                        o_ref, keys_ref, m_ref, l_ref, acc_ref):
    tk = tq
    qt = pl.program_id(1)
    nk = qt + 1
    krow = lax.broadcasted_iota(I32, (tk, tq), 0)
    qcol = lax.broadcasted_iota(I32, (tk, tq), 1)

    def score_body(kt, carry):
        kic = ki_ref[0, kt]
        sc = jnp.zeros((tk, tq), F32)
        for h in range(IDX_HEADS):
            s = _dot(kic, qiT_ref[0, 0, h])
            sc = sc + jnp.maximum(s, 0.0) * (wiT_ref[0, h:h + 1, :] * IDX_HEADS ** -0.5)
        causal = (kt * tk + krow) <= (qt * tq + qcol)
        keys_ref[kt] = jnp.where(causal, _monotone_key(sc), INT_MIN)
        return carry

    lax.fori_loop(0, nk, score_body, 0)

    def count_ge(cand):
        def cnt_body(kt, c):
            hit = jnp.where(keys_ref[kt] >= cand, 1, 0)
            return c + jnp.sum(hit.reshape(tk // SUBLANES, SUBLANES, tq), axis=0)
        c = lax.fori_loop(0, nk, cnt_body, jnp.zeros((SUBLANES, tq), I32))
        return jnp.sum(c, axis=0, keepdims=True)

    thr = jnp.maximum(_topk_threshold(count_ge, (1, tq), kk), INT_MIN + 1)

    m_ref[...] = jnp.full(m_ref.shape, NEG, F32)
    l_ref[...] = jnp.zeros(l_ref.shape, F32)
    acc_ref[...] = jnp.zeros(acc_ref.shape, F32)

    def kv_step(kt, bias_of, thr_k):
        mask = keys_ref[kt] >= thr_k
        for h in range(N_HEADS):
            s = _dot(kh_ref[0, h, kt], qT_ref[0, 0, h]) + bias_of(h)
            m_old = m_ref[h]
            m_new = jnp.maximum(m_old, jnp.max(jnp.where(mask, s, NEG), axis=0, keepdims=True))
            p = jnp.where(mask, jnp.exp(s - m_new), 0.0)
            alpha = jnp.exp(m_old - m_new)
            l_ref[h] = alpha * l_ref[h] + jnp.sum(p, axis=0, keepdims=True)
            acc_ref[h] = alpha * acc_ref[h] + _dot(vT_ref[0, kt, h], p.astype(BF))
            m_ref[h] = m_new

    def far_body(kt, carry):
        kv_step(kt, lambda h: cb_ref[h], thr)
        return carry

    lax.fori_loop(0, jnp.maximum(nk - n_toep, 0), far_body, 0)
    for o in reversed(range(n_toep)):
        thr_o = jnp.where(qt >= o, thr, INT_MAX)
        kv_step(jnp.maximum(qt - o, 0), lambda h, o=o: toep_ref[o, h], thr_o)
    for h in range(N_HEADS):
        o_ref[0, h] = acc_ref[h] / l_ref[h]


def _attn_prompt(cb, qT, qiT, wiT, kh, vT, ki, toep, tq, kk):
    b, nq = qT.shape[:2]
    s = nq * tq
    n_toep = toep.shape[0]
    q_tile = lambda n: pl.BlockSpec((1, 1, N_HEADS, n, tq), lambda i, j: (i, j, 0, 0, 0))
    return pl.pallas_call(
        functools.partial(_attn_prompt_kernel, tq, n_toep, kk),
        grid=(b, nq),
        in_specs=[pl.BlockSpec(memory_space=pltpu.SMEM),
                  q_tile(HEAD_DIM), q_tile(IDX_DIM),
                  pl.BlockSpec((1, IDX_HEADS, tq), lambda i, j: (i, 0, j)),
                  pl.BlockSpec((1, N_HEADS, nq, tq, HEAD_DIM), lambda i, j: (i, 0, 0, 0, 0)),
                  pl.BlockSpec((1, nq, N_HEADS, HEAD_DIM, tq), lambda i, j: (i, 0, 0, 0, 0)),
                  pl.BlockSpec((1, nq, tq, IDX_DIM), lambda i, j: (i, 0, 0, 0)),
                  _full(toep.shape)],
        out_specs=pl.BlockSpec((1, N_HEADS, HEAD_DIM, tq), lambda i, j: (i, 0, 0, j)),
        out_shape=jax.ShapeDtypeStruct((b, N_HEADS, HEAD_DIM, s), F32),
        scratch_shapes=[pltpu.VMEM((nq, tq, tq), I32),
                        pltpu.VMEM((N_HEADS, 1, tq), F32), pltpu.VMEM((N_HEADS, 1, tq), F32),
                        pltpu.VMEM((N_HEADS, HEAD_DIM, tq), F32)],
        compiler_params=_params(2),
        name="attn_prompt",
    )(cb, qT, qiT, wiT, kh, vT, ki, toep)


PAGES_PER_STEP = 16


def _page_specs(block, n_pages):
    def spec(r):
        return pl.BlockSpec(block, lambda b, g, pt: (0, pt[b * n_pages + g * PAGES_PER_STEP + r]) + (0,) * (len(block) - 2))
    return [spec(r) for r in range(PAGES_PER_STEP)]


def _samp_scores_kernel(pt_ref, qi_ref, wi_ref, *refs):
    pages, o_ref = refs[:PAGES_PER_STEP], refs[PAGES_PER_STEP]
    qi = qi_ref[0]
    w = wi_ref[0] * IDX_HEADS ** -0.5
    rows = []
    for r in range(PAGES_PER_STEP):
        s = _dot(qi, pages[r][0, 0].astype(BF))
        rows.append(jnp.sum(jnp.maximum(s, 0.0) * w, axis=0, keepdims=True))
    o_ref[0, 0] = jnp.concatenate(rows, axis=0)


def _samp_scores(page_table_flat, qi, wi, cache_idx_t, n_pages):
    db = qi.shape[0]
    ng = n_pages // PAGES_PER_STEP
    grid_spec = pltpu.PrefetchScalarGridSpec(
        num_scalar_prefetch=1,
        grid=(db, ng),
        in_specs=[pl.BlockSpec((1, IDX_HEADS, IDX_DIM), lambda b, g, pt: (b, 0, 0)),
                  pl.BlockSpec((1, IDX_HEADS, 1), lambda b, g, pt: (b, 0, 0))]
                 + _page_specs((1, 1, IDX_DIM, PAGE_SIZE), n_pages),
        out_specs=pl.BlockSpec((1, 1, PAGES_PER_STEP, PAGE_SIZE), lambda b, g, pt: (b, g, 0, 0)),
    )
    return pl.pallas_call(
        _samp_scores_kernel,
        grid_spec=grid_spec,
        out_shape=jax.ShapeDtypeStruct((db, ng, PAGES_PER_STEP, PAGE_SIZE), F32),
        compiler_params=_params(2),
        name="samp_scores",
    )(page_table_flat, qi, wi, *([cache_idx_t] * PAGES_PER_STEP))


SEQ_PER_STEP = 8


def _total(x):
    return jnp.sum(jnp.sum(x, axis=0, keepdims=True), axis=1, keepdims=True)


def _ind(m):
    return jnp.where(m, 1.0, 0.0)


def _samp_select_kernel(kk, sc_ref, qi_ref, ki8_ref, kw_ref, tri_ref, lt_ref, sel_ref, nsel_ref,
                        keys_ref, thr_ref, knew_ref):
    key_new = []
    for r in range(SEQ_PER_STEP):
        prod = qi_ref[r].astype(F32) * ki8_ref[r].astype(F32)
        kwv = kw_ref[r]
        s_new = jnp.zeros((1, 1), F32)
        for h in range(IDX_HEADS):
            s = jnp.sum(prod[:, h * IDX_DIM:(h + 1) * IDX_DIM], axis=1, keepdims=True)
            s_new = s_new + jnp.maximum(s, 0.0) * (kwv[:, IDX_DIM + h:IDX_DIM + h + 1] * IDX_HEADS ** -0.5)
        key_new.append(_monotone_key(s_new))
        keys_ref[r] = _monotone_key(sc_ref[r])

    def bit_body(bi, thr_us):
        bit = jnp.left_shift(jnp.int32(1), 31 - bi)
        out = []
        for r in range(SEQ_PER_STEP):
            cand_u = thr_us[r] | bit
            cand = cand_u ^ INT_MIN
            cnt = _total(jnp.where(keys_ref[r] >= cand, 1, 0)) + jnp.where(key_new[r] >= cand, 1, 0)
            out.append(jnp.where(cnt >= kk, cand_u, thr_us[r]))
        return tuple(out)

    thr_us = lax.fori_loop(0, 32, bit_body, tuple(jnp.zeros((1, 1), I32) for _ in range(SEQ_PER_STEP)))
    for r in range(SEQ_PER_STEP):
        thr_ref[r] = jnp.broadcast_to(thr_us[r] ^ INT_MIN, (1, LANES))
        knew_ref[r] = jnp.broadcast_to(key_new[r], (1, LANES))

    def prefix(x):
        incl = _dot(x.astype(BF), tri_ref[...])
        page_tot = jnp.broadcast_to(incl[:, PAGE_SIZE - 1:PAGE_SIZE], incl.shape)
        return incl - x + _dot(lt_ref[...], page_tot.astype(BF))

    def seq_body(r, carry):
        keys = keys_ref[r]
        thr = thr_ref[r][:, :1]
        kn = knew_ref[r][:, :1]
        gt = keys > thr
        eq = keys == thr
        eq_f = _ind(eq)
        need = kk - (_total(_ind(gt)) + _ind(kn > thr))
        sel = jnp.where(gt, 1.0, jnp.where(eq, _ind(prefix(eq_f) < need), 0.0))
        new_sel = jnp.where(kn > thr, 1.0, jnp.where(kn == thr, _ind(_total(eq_f) < need), 0.0))
        sel_ref[r] = sel
        nsel_ref[r] = jnp.broadcast_to(new_sel, (1, LANES))
        return carry

    lax.fori_loop(0, SEQ_PER_STEP, seq_body, 0)


def _samp_select(scores, qib, ki8, kw, kk):
    db, n_pages, _ = scores.shape
    spb = SEQ_PER_STEP
    assert db % spb == 0
    pos = np.arange(PAGE_SIZE)
    tri = jnp.asarray(pos[:, None] <= pos[None, :], BF)
    pg = np.arange(n_pages)
    lt = jnp.asarray(pg[None, :] < pg[:, None], BF)
    per_seq = lambda n: pl.BlockSpec((spb, 1, n), lambda i: (i, 0, 0))
    return pl.pallas_call(
        functools.partial(_samp_select_kernel, kk),
        grid=(db // spb,),
        in_specs=[pl.BlockSpec((spb, n_pages, PAGE_SIZE), lambda i: (i, 0, 0)),
                  per_seq(qib.shape[2]), per_seq(ki8.shape[2]), per_seq(kw.shape[2]),
                  _full(tri.shape), _full(lt.shape)],
        out_specs=[pl.BlockSpec((spb, n_pages, PAGE_SIZE), lambda i: (i, 0, 0)), per_seq(LANES)],
        out_shape=[jax.ShapeDtypeStruct((db, n_pages, PAGE_SIZE), F32),
                   jax.ShapeDtypeStruct((db, 1, LANES), F32)],
        scratch_shapes=[pltpu.VMEM((spb, n_pages, PAGE_SIZE), I32),
                        pltpu.VMEM((spb, 1, LANES), I32), pltpu.VMEM((spb, 1, LANES), I32)],
        compiler_params=_params(1),
        name="samp_select",
    )(scores, qib, ki8, kw, tri, lt)


def _samp_attn_kernel(pt_ref, q_ref, qt_ref, sel_ref, bias_ref, bnew_ref, kn_ref, vnt_ref, *refs):
    k_pages = refs[:PAGES_PER_STEP]
    v_pages = refs[PAGES_PER_STEP:2 * PAGES_PER_STEP]
    o_ref, m_ref, l_ref, acc_ref, qb_ref = refs[2 * PAGES_PER_STEP:]
    g = pl.program_id(1)

    @pl.when(g == 0)
    def _():
        m_ref[...] = jnp.full(m_ref.shape, NEG, F32)
        l_ref[...] = jnp.zeros(l_ref.shape, F32)
        acc_ref[...] = jnp.zeros(acc_ref.shape, F32)
        for h in range(N_HEADS):
            qb_ref[h] = jnp.broadcast_to(qt_ref[0][:, h:h + 1], (HEAD_DIM, PAGE_SIZE))

    logits, masks = [], []
    for r in range(PAGES_PER_STEP):
        rows = [jnp.sum(k_pages[r][0, 0, h] * qb_ref[h], axis=0, keepdims=True) for h in range(N_HEADS)]
        mask = sel_ref[0, r:r + 1, :] > 0.5
        masks.append(mask)
        logits.append(jnp.where(mask, jnp.concatenate(rows, axis=0) + bias_ref[0, r], NEG))
    m_old = m_ref[...]
    m_new = m_old
    for s in logits:
        m_new = jnp.maximum(m_new, jnp.max(s, axis=1, keepdims=True))
    alpha = jnp.exp(m_old - m_new)
    probs = [jnp.where(mask, jnp.exp(s - m_new), 0.0) for s, mask in zip(logits, masks)]
    l = alpha * l_ref[...]
    for p in probs:
        l = l + jnp.sum(p, axis=1, keepdims=True)
    l_ref[...] = l
    m_ref[...] = m_new
    for h in range(N_HEADS):
        acc = acc_ref[h] * alpha[h:h + 1, :]
        for r in range(PAGES_PER_STEP):
            acc = acc + v_pages[r][0, 0, h] * probs[r][h:h + 1, :]
        acc_ref[h] = acc

    @pl.when(g == pl.num_programs(1) - 1)
    def _():
        s_new = jnp.sum(kn_ref[0] * q_ref[0], axis=1, keepdims=True) + bnew_ref[0]
        m_fin = jnp.maximum(m_new, s_new)
        a_fin = jnp.exp(m_new - m_fin)
        p_new = jnp.exp(s_new - m_fin)
        l_fin = a_fin * l + p_new
        cols = []
        for h in range(N_HEADS):
            past = jnp.sum(acc_ref[h], axis=1, keepdims=True) * a_fin[h:h + 1, :]
            cols.append((past + vnt_ref[0][:, h:h + 1] * p_new[h:h + 1, :]) / l_fin[h:h + 1, :])
        o_ref[0] = jnp.concatenate(cols, axis=1)


def _samp_attn(page_table_flat, q, qt, sel, bias_t, bias_new, kn, vnt, cache_k_t, cache_v_t, n_pages):
    db = q.shape[0]
    ng = n_pages // PAGES_PER_STEP
    per_seq = lambda shape: pl.BlockSpec((1,) + shape, lambda b, g, pt: (b,) + (0,) * len(shape))
    page_block = (1, 1, N_HEADS, HEAD_DIM, PAGE_SIZE)
    grid_spec = pltpu.PrefetchScalarGridSpec(
        num_scalar_prefetch=1,
        grid=(db, ng),
        in_specs=[per_seq((N_HEADS, HEAD_DIM)), per_seq((HEAD_DIM, N_HEADS)),
                  pl.BlockSpec((1, PAGES_PER_STEP, PAGE_SIZE), lambda b, g, pt: (b, g, 0)),
                  pl.BlockSpec((1, PAGES_PER_STEP, N_HEADS, PAGE_SIZE), lambda b, g, pt: (g, 0, 0, 0)),
                  per_seq((N_HEADS, 1)), per_seq((N_HEADS, HEAD_DIM)), per_seq((HEAD_DIM, N_HEADS))]
                 + _page_specs(page_block, n_pages) + _page_specs(page_block, n_pages),
        out_specs=per_seq((HEAD_DIM, N_HEADS)),
        scratch_shapes=[pltpu.VMEM((N_HEADS, 1), F32), pltpu.VMEM((N_HEADS, 1), F32),
                        pltpu.VMEM((N_HEADS, HEAD_DIM, PAGE_SIZE), F32),
                        pltpu.VMEM((N_HEADS, HEAD_DIM, PAGE_SIZE), F32)],
    )
    return pl.pallas_call(
        _samp_attn_kernel,
        grid_spec=grid_spec,
        out_shape=jax.ShapeDtypeStruct((db, HEAD_DIM, N_HEADS), F32),
        compiler_params=_params(2),
        name="samp_attn",
    )(page_table_flat, q, qt, sel, bias_t, bias_new, kn, vnt,
      *([cache_k_t] * PAGES_PER_STEP), *([cache_v_t] * PAGES_PER_STEP))


def _merge_kernel(per_row, x_ref, yc_ref, at_ref, sgc_ref, sga_ref, mod_ref, lng_ref, lnb_ref,
                  wco_ref, wao_ref, wout_ref, g2_ref, wpq_ref, kbd_ref,
                  xm_ref, h2t_ref, st_ref):
    yc = yc_ref[...]
    mu = jnp.mean(yc, axis=-1, keepdims=True)
    yd = yc - mu
    ln = yd * lax.rsqrt(jnp.mean(yd * yd, axis=-1, keepdims=True) + EPS) * lng_ref[...] + lnb_ref[...]
    act = ln * jax.nn.sigmoid(ln)
    conv_out = _dot(act.astype(BF), wco_ref[...])
    attn_out = _dot(at_ref[0].T.astype(BF), wao_ref[...])
    merged = sgc_ref[...] * conv_out + sga_ref[...] * attn_out
    xm = x_ref[...] + _mod(mod_ref, 2, per_row) * _dot(merged.astype(BF), wout_ref[...])
    xm_ref[...] = xm
    rs = lax.rsqrt(jnp.mean(xm * xm, axis=-1, keepdims=True) + EPS)
    h2 = (xm * rs) * g2_ref[...] * (1.0 + _mod(mod_ref, 4, per_row)) + _mod(mod_ref, 3, per_row)
    h2t_ref[...] = h2.T.astype(BF)
    pq = _dot(h2.astype(BF), wpq_ref[...])
    st_ref[...] = _dot_nt(kbd_ref[...], pq.astype(BF))


def _merge(x, yc, at, sgc, sga, mod, per_row, rows_per_seq, tm, wts):
    t = x.shape[0]
    row = lambda n: pl.BlockSpec((tm, n), lambda i: (i, 0))
    col = lambda n: pl.BlockSpec((n, tm), lambda i: (0, i))
    n_scores = PEER_HEADS * 2 * PEER_KEYS
    return pl.pallas_call(
        functools.partial(_merge_kernel, per_row),
        grid=(t // tm,),
        in_specs=[row(D_MODEL), row(CONV_DIM),
                  pl.BlockSpec((1, ATTN_DIM, tm), lambda i: (i // (at.shape[2] // tm), 0, i % (at.shape[2] // tm))),
                  row(D_MODEL), row(D_MODEL),
                  _mod_spec(per_row, tm, rows_per_seq)] + [_full(w.shape) for w in wts],
        out_specs=[row(D_MODEL), col(D_MODEL), col(n_scores)],
        out_shape=[jax.ShapeDtypeStruct((t, D_MODEL), F32),
                   jax.ShapeDtypeStruct((D_MODEL, t), BF),
                   jax.ShapeDtypeStruct((n_scores, t), F32)],
        compiler_params=_params(1),
        name="merge",
    )(x, yc, at, sgc, sga, mod, *wts)


N_EXTRACT = PEER_TOPK + 1


def _top_extract(cur, n):
    outs = []
    for _ in range(n):
        mx = jnp.max(cur, axis=0, keepdims=True)
        outs.append(mx)
        cur = jnp.where(cur >= mx, -jnp.inf, cur)
    return outs


RANK_NONE = float(PEER_KEYS - 1)


def _peer_select_kernel(lb, st_ref, r1_ref, e1_ref, n_ref, al_ref):
    tl = st_ref.shape[1]

    def head_body(h, carry):
        base = pl.multiple_of(h * 2 * PEER_KEYS, 2 * PEER_KEYS)
        s0 = st_ref[pl.ds(base, PEER_KEYS), :]
        s1 = st_ref[pl.ds(base + PEER_KEYS, PEER_KEYS), :]
        a0 = _top_extract(s0, N_EXTRACT)
        a1 = []
        cur = s1
        rank1 = jnp.full(s1.shape, RANK_NONE, F32)
        for k in range(N_EXTRACT):
            mx = jnp.max(cur, axis=0, keepdims=True)
            a1.append(mx)
            hit = cur >= mx
            if k < PEER_TOPK:
                rank1 = jnp.where(hit, float(k), rank1)
            cur = jnp.where(hit, -jnp.inf, cur)
        pad = [jnp.full((1, tl), -jnp.inf, F32)] * (3 * SUBLANES - N_EXTRACT)
        a1_all = jnp.concatenate(a1 + pad, axis=0)
        cands = [a0[0] + a1_all] + [a0[k] + a1_all[:SUBLANES] for k in range(1, N_EXTRACT)]
        top = _top_extract(jnp.concatenate(cands, axis=0), N_EXTRACT)
        thr = 0.5 * (top[PEER_TOPK - 1] + top[PEER_TOPK])
        z = jnp.zeros_like(thr)
        for k in range(PEER_TOPK):
            z = z + jnp.exp(top[k] - top[0])
        tau = thr - s0
        cnt = jnp.zeros(s0.shape, F32)
        for l in range(PEER_TOPK):
            cnt = cnt + jnp.where(a1[l] >= tau, 1.0, 0.0)
        e1 = jnp.exp(s1 - a1[0])
        al = jnp.exp(s0 - a0[0]) / z
        for tb in range(tl // lb):
            ts = slice(tb * lb, (tb + 1) * lb)
            pack = lambda v: pltpu.bitcast(v[:, ts].astype(BF), I32)
            r1_ref[h, tb] = pack(rank1)
            e1_ref[h, tb] = pack(e1)
            n_ref[h, tb] = pack(cnt)
            al_ref[h, tb] = pack(al)
        return carry

    lax.fori_loop(0, PEER_HEADS, head_body, 0)


def _peer_select(st, tl):
    t = st.shape[1]
    lb = min(LANES, t)
    spec = pl.BlockSpec((PEER_HEADS, tl // lb, PEER_KEYS // 2, lb), lambda i: (0, i, 0, 0))
    shp = jax.ShapeDtypeStruct((PEER_HEADS, t // lb, PEER_KEYS // 2, lb), I32)
    return pl.pallas_call(
        functools.partial(_peer_select_kernel, lb),
        grid=(t // tl,),
        in_specs=[pl.BlockSpec((st.shape[0], tl), lambda i: (0, i))],
        out_specs=[spec] * 4,
        out_shape=[shp] * 4,
        compiler_params=_params(1),
        name="peer_select",
    )(st)


PEER_ROWS_PER_STEP = 2 * SUBLANES


def _peer_main_kernel(per_row, h2t_ref, u_ref, vt_ref, r1_ref, e1_ref, n_ref, al_ref, xm_ref, mod_ref,
                      o_ref, acc_ref, a_ref, g_ref):
    j = pl.program_id(1)
    tm = h2t_ref.shape[1]
    n_tb, _, lb = r1_ref.shape[1:]

    @pl.when(j == 0)
    def _():
        acc_ref[...] = jnp.zeros(acc_ref.shape, F32)

    i0 = pl.multiple_of(j * SUBLANES, SUBLANES)
    h2t = h2t_ref[...]
    unit = 2 * PEER_KEYS
    n_units = PEER_ROWS_PER_STEP // 2

    def activations(un):
        a_ref[un % 2] = _dot(u_ref[un * unit:(un + 1) * unit, :], h2t)

    activations(0)
    for un in range(n_units):
        rows = slice(un * unit, (un + 1) * unit)
        if un + 1 < n_units:
            activations(un + 1)
        for tb in range(n_tb):
            ts = slice(tb * lb, (tb + 1) * lb)
            w = [jnp.zeros((PEER_KEYS, lb), BF), jnp.zeros((PEER_KEYS, lb), BF)]
            for h in range(PEER_HEADS):
                cnt = pltpu.bitcast(n_ref[h, tb, pl.ds(i0, SUBLANES), :], BF)
                al = pltpu.bitcast(al_ref[h, tb, pl.ds(i0, SUBLANES), :], BF)
                r1 = pltpu.bitcast(r1_ref[h, tb], BF)
                e1 = pltpu.bitcast(e1_ref[h, tb], BF)
                for r in range(2):
                    ii = 2 * un + r
                    cnt_b = jnp.broadcast_to(cnt[ii:ii + 1], (PEER_KEYS, lb))
                    al_b = jnp.broadcast_to(al[ii:ii + 1], (PEER_KEYS, lb))
                    w[r] = w[r] + jnp.where(r1 < cnt_b, e1 * al_b, 0)
            for r in range(2):
                a = a_ref[un % 2, r * PEER_KEYS:(r + 1) * PEER_KEYS, ts]
                gelu = 0.5 * a * (1.0 + lax.erf(a * (2.0 ** -0.5)))
                g_ref[un * unit + r * PEER_KEYS:un * unit + (r + 1) * PEER_KEYS, ts] = w[r] * gelu.astype(BF)
        acc_ref[...] += _dot(vt_ref[:, rows], g_ref[rows, :])

    @pl.when(j == pl.num_programs(1) - 1)
    def _():
        o_ref[...] = xm_ref[...] + _mod(mod_ref, 5, per_row) * acc_ref[...].T


def _peer_main(h2t, u_bf, vt_bf, r1, e1, cnt, al, xm, mod, per_row, rows_per_seq, tm):
    t = xm.shape[0]
    ec = PEER_ROWS_PER_STEP * PEER_KEYS
    n_e = u_bf.shape[0]
    lb = r1.shape[3]
    sel = pl.BlockSpec((PEER_HEADS, tm // lb, PEER_KEYS // 2, lb), lambda i, j: (0, i, 0, 0))
    if per_row:
        mod_spec = pl.BlockSpec((6, tm, D_MODEL), lambda i, j: (0, i, 0))
    else:
        tiles_per_seq = rows_per_seq // tm
        mod_spec = pl.BlockSpec((1, 6, D_MODEL), lambda i, j: (i // tiles_per_seq, 0, 0))
    return pl.pallas_call(
        functools.partial(_peer_main_kernel, per_row),
        grid=(t // tm, n_e // ec),
        in_specs=[pl.BlockSpec((D_MODEL, tm), lambda i, j: (0, i)),
                  pl.BlockSpec((ec, D_MODEL), lambda i, j: (j, 0)),
                  pl.BlockSpec((D_MODEL, ec), lambda i, j: (0, j)),
                  sel, sel, sel, sel,
                  pl.BlockSpec((tm, D_MODEL), lambda i, j: (i, 0)),
                  mod_spec],
        out_specs=pl.BlockSpec((tm, D_MODEL), lambda i, j: (i, 0)),
        out_shape=jax.ShapeDtypeStruct((t, D_MODEL), F32),
        scratch_shapes=[pltpu.VMEM((D_MODEL, tm), F32), pltpu.VMEM((2, 2 * PEER_KEYS, tm), F32),
                        pltpu.VMEM((ec, tm), BF)],
        compiler_params=_params(2),
        name="peer_main",
    )(h2t, u_bf, vt_bf, r1, e1, cnt, al, xm, mod)


def _tile(n, pref):
    return pref if n % pref == 0 else n


def kernel(x_prompt, x_sample, cache_k, cache_v, cache_idx_k, state_conv, page_table, c_prompt, c_sample,
           w_ada, b_ada, norm1_g, w_in, q_norm_g, k_norm_g, rel_bias, w_dw, b_dw, conv_ln_g, conv_ln_b,
           w_conv_o, w_attn_o, w_out, norm2_g, w_peer_q, peer_keys, peer_u, peer_v):
    b, s, _ = x_prompt.shape
    db, ds, _ = x_sample.shape
    n_pages = page_table.shape[1]
    past = n_pages * PAGE_SIZE
    assert w_ada.shape[0] == 1, "one layer"
    assert ds == 1, "one new token per sample sequence"
    assert n_pages % PAGES_PER_STEP == 0
    tp = b * s

    w = w_in[0]
    pts = np.cumsum([0, 2 * CONV_DIM, ATTN_DIM, ATTN_DIM, ATTN_DIM, IDX_HEADS * IDX_DIM, IDX_DIM, IDX_HEADS,
                     D_MODEL, D_MODEL])
    seg = lambda i: w[:, pts[i]:pts[i + 1]].astype(BF)
    w_kw = jnp.pad(w[:, pts[5]:pts[7]], ((0, 0), (0, LANES - IDX_DIM - IDX_HEADS))).astype(BF)
    head_of = np.arange(ATTN_DIM) // HEAD_DIM
    bd = jnp.asarray(head_of[:, None] == head_of[None, :], BF)
    row = lambda v: v.reshape(1, -1)
    in_wts = [row(norm1_g[0]), seg(0), seg(1), seg(2), seg(3), seg(4), w_kw, seg(7), seg(8),
              row(jnp.tile(q_norm_g[0], N_HEADS)), row(jnp.tile(k_norm_g[0], N_HEADS)), bd]
    n_half = PEER_HEADS * 2
    kb16 = peer_keys[0].reshape(n_half, PEER_KEYS, PEER_HALF)
    kbd_t = (jnp.eye(n_half, dtype=F32)[:, None, :, None] * kb16[:, :, None, :]).reshape(
        n_half * PEER_KEYS, n_half * PEER_HALF).astype(BF)
    merge_wts = [row(conv_ln_g[0]), row(conv_ln_b[0]), w_conv_o[0].astype(BF), w_attn_o[0].astype(BF),
                 w_out[0].astype(BF), row(norm2_g[0]), w_peer_q[0].astype(BF), kbd_t]
    u_bf = peer_u[0].astype(BF)
    vt_bf = peer_v[0].T.astype(BF)

    mod = _ada(jnp.concatenate([c_prompt, c_sample], axis=0), w_ada[0].astype(BF), row(b_ada[0]))
    mod_p = mod[:b].reshape(b, 6, D_MODEL)
    mod_s = mod[b:].reshape(db, 6, D_MODEL).transpose(1, 0, 2)

    tq = _tile(s, 256)
    nq = s // tq
    glu_p, sgc_p, sga_p, kT_p, vT32_p, kiT_p, qT_p, qiT_p, vT_p, kh_p, ki_p, wiT_p = _inproj(
        x_prompt.reshape(tp, D_MODEL), mod_p, False, s, tq, in_wts)
    glu_s, sgc_s, sga_s, k_s, v_s, kw_s, qb_s, qib_s = _inproj(
        x_sample.reshape(db, D_MODEL), mod_s, True, 1, db, in_wts)

    yc_p = _conv_prompt(glu_p.reshape(b, s, CONV_DIM), w_dw[0], row(b_dw[0]), _tile(s, 256)).reshape(tp, CONV_DIM)
    yc_s = _conv_sample(state_conv[0].transpose(1, 0, 2), glu_s, w_dw[0], row(b_dw[0]))

    bucket =_rel_bucket_np(np.arange(max(s, past + 1)))
    n_toep = nq
    while n_toep > 0 and len(set(bucket[max((n_toep - 1) * tq - (tq - 1), 0):n_toep * tq])) == 1 \
            and bucket[(n_toep - 1) * tq] == bucket[s - 1]:
        n_toep -= 1
    n_toep = max(n_toep, 1)

    def bias_of_bucket(bkt):
        onehot = jnp.asarray(np.eye(REL_BUCKETS, dtype=np.float32)[bkt.reshape(-1)])
        return jnp.dot(onehot, rel_bias, precision=lax.Precision.HIGHEST).reshape(bkt.shape + (N_HEADS,))

    width = 2 * tq + 1
    dist = (np.arange(n_toep)[:, None] - 1) * tq + np.arange(width)[None, :]
    table = bias_of_bucket(_rel_bucket_np(dist)).transpose(0, 2, 1)
    skew = jnp.tile(table, (1, 1, tq))[:, :, :tq * (width - 1)].reshape(n_toep, N_HEADS, tq, width - 1)
    toep = skew[:, :, :, tq:]
    cb = rel_bias[int(bucket[s - 1])]

    kk_p = min(IDX_TOPK_MAX, s // 4)
    split_heads = lambda a: a.reshape(b, nq, N_HEADS, -1, tq)
    at_p = _attn_prompt(cb, split_heads(qT_p), split_heads(qiT_p), wiT_p, kh_p, split_heads(vT_p), ki_p,
                        toep, tq, kk_p)
    at_p = at_p.reshape(b, ATTN_DIM, s)

    kk_s = min(IDX_TOPK_MAX, (past + ds) // 4)
    pt_flat = page_table.reshape(-1)
    sc = _samp_scores(pt_flat, qib_s.reshape(db, IDX_HEADS, IDX_DIM),
                      kw_s[:, IDX_DIM:IDX_DIM + IDX_HEADS].reshape(db, IDX_HEADS, 1),
                      cache_idx_k.transpose(0, 1, 3, 2), n_pages)
    ki8 = jnp.tile(kw_s[:, :IDX_DIM].astype(BF), (1, IDX_HEADS))
    sel_s, new_sel = _samp_select(sc.reshape(db, n_pages, PAGE_SIZE), qib_s[:, None, :], ki8[:, None, :],
                                  kw_s[:, None, :], kk_s)
    pos = np.arange(past).reshape(n_pages // PAGES_PER_STEP, PAGES_PER_STEP, PAGE_SIZE)
    bias_t = bias_of_bucket(_rel_bucket_np(past - pos)).transpose(0, 1, 3, 2)
    bias_new = jnp.where(new_sel[:, 0, :1] > 0.0, rel_bias[int(_rel_bucket_np(np.zeros((), np.int64)))][None], NEG)
    heads = lambda a: a.reshape(db, N_HEADS, HEAD_DIM)
    q_s = heads(qb_s.astype(F32))
    at_s = _samp_attn(pt_flat, q_s, q_s.transpose(0, 2, 1), sel_s, bias_t, bias_new.reshape(db, N_HEADS, 1),
                      heads(k_s), heads(v_s).transpose(0, 2, 1),
                      cache_k.transpose(0, 1, 3, 4, 2), cache_v.transpose(0, 1, 3, 4, 2), n_pages)
    at_s = at_s.transpose(2, 1, 0).reshape(1, ATTN_DIM, db)

    def tail(x, yc, at, sgc, sga, mod_g, per_row, rows_per_seq, tm_merge, tl_sel, tm_peer):
        xm, h2t, st = _merge(x, yc, at, sgc, sga, mod_g, per_row, rows_per_seq, tm_merge, merge_wts)
        s1, e1, tau, al = _peer_select(st, tl_sel)
        return _peer_main(h2t, u_bf, vt_bf, s1, e1, tau, al, xm, mod_g, per_row, rows_per_seq, tm_peer)

    y_p = tail(x_prompt.reshape(tp, D_MODEL), yc_p, at_p, sgc_p, sga_p, mod_p, False, s,
               _tile(s, 256), _tile(tp, 256), _tile(s, 512))
    y_s = tail(x_sample.reshape(db, D_MODEL), yc_s, at_s, sgc_s, sga_s, mod_s, True, 1, db, db, db)

    glu_p3 = glu_p.reshape(b, s, CONV_DIM)
    conv_s = jnp.concatenate([state_conv[0], glu_s[:, None, :]], axis=1)[:, -CONV_STATE:]
    seq_major = lambda a: a.reshape(b, N_HEADS, HEAD_DIM, s).transpose(0, 3, 1, 2)[None]
    return (y_p.reshape(b, s, D_MODEL), y_s.reshape(db, ds, D_MODEL),
            seq_major(kT_p), seq_major(vT32_p),
            kiT_p.transpose(0, 2, 1)[None], glu_p3[None, :, -CONV_STATE:],
            k_s.reshape(1, db, ds, N_HEADS, HEAD_DIM), v_s.reshape(1, db, ds, N_HEADS, HEAD_DIM),
            kw_s[:, :IDX_DIM].reshape(1, db, ds, IDX_DIM), conv_s[None])
```

```python
import functools
import math

import numpy as np
import jax
import jax.numpy as jnp
from jax import lax
from jax.experimental import pallas as pl
from jax.experimental.pallas import tpu as pltpu

F32 = jnp.float32
BF = jnp.bfloat16
I32 = jnp.int32

D_MODEL = 1024
N_HEADS = 8
HEAD_DIM = 64
ATTN_DIM = N_HEADS * HEAD_DIM
IDX_HEADS = 8
IDX_DIM = 64
IDX_TOPK_MAX = 256
REL_BUCKETS = 32
REL_MAX_DIST = 128
CONV_DIM = 512
CONV_WIDTH = 31
CONV_STATE = CONV_WIDTH - 1
PEER_HEADS = 8
PEER_KEYS = 128
PEER_HALF = 64
PEER_TOPK = 16
PAGE_SIZE = 128
EPS = 1e-6

NEG = -1e30
INT_MIN = -(2 ** 31)
INT_MAX = 2 ** 31 - 1
LANES = 128
SUBLANES = 8
VMEM_LIMIT = 52 * 1024 * 1024


def _params(n_axes):
    return pltpu.CompilerParams(dimension_semantics=("arbitrary",) * n_axes,
                                vmem_limit_bytes=VMEM_LIMIT)


def _full(shape):
    zeros = (0,) * len(shape)
    return pl.BlockSpec(shape, lambda *_: zeros)


def _mod(mod_ref, k, per_row):
    return mod_ref[k] if per_row else mod_ref[0, k:k + 1, :]


def _mod_spec(per_row, tm, rows_per_seq):
    if per_row:
        return pl.BlockSpec((6, tm, D_MODEL), lambda i: (0, i, 0))
    tiles_per_seq = rows_per_seq // tm
    return pl.BlockSpec((1, 6, D_MODEL), lambda i: (i // tiles_per_seq, 0, 0))


def _dot(a, b):
    return jnp.dot(a, b, preferred_element_type=F32)


def _dot_nt(a, b):
    return lax.dot_general(a, b, (((1,), (1,)), ((), ())), preferred_element_type=F32)


def _monotone_key(x):
    bits = lax.bitcast_convert_type(x, I32)
    return bits ^ ((bits >> 31) & INT_MAX)


def _ada_kernel(c_ref, w_ref, b_ref, o_ref):
    c = c_ref[...]
    a = (c * jax.nn.sigmoid(c)).astype(BF)
    o_ref[...] = _dot(a, w_ref[...]) + b_ref[...]


def _ada(c, w_bf, b):
    rows = c.shape[0]
    n = w_bf.shape[1]
    nb = 1536
    return pl.pallas_call(
        _ada_kernel,
        grid=(n // nb,),
        in_specs=[_full((rows, D_MODEL)),
                  pl.BlockSpec((D_MODEL, nb), lambda j: (0, j)),
                  pl.BlockSpec((1, nb), lambda j: (0, j))],
        out_specs=pl.BlockSpec((rows, nb), lambda j: (0, j)),
        out_shape=jax.ShapeDtypeStruct((rows, n), F32),
        compiler_params=_params(1),
        name="ada",
    )(c, w_bf, b)


def _inproj_kernel(per_row, x_ref, mod_ref, g1_ref, wglu_ref, wq_ref, wk_ref, wv_ref, wqi_ref,
                   wkw_ref, wgc_ref, wga_ref, gq_ref, gk_ref, bd_ref,
                   glu_ref, sgc_ref, sga_ref, *attn_refs):
    x = x_ref[...]
    shift = _mod(mod_ref, 0, per_row)
    scale = _mod(mod_ref, 1, per_row)
    rs = lax.rsqrt(jnp.mean(x * x, axis=-1, keepdims=True) + EPS)
    h = (x * rs) * g1_ref[...] * (1.0 + scale) + shift
    hb = h.astype(BF)

    def head_norm(z, g_ref):
        sq = z * z
        hi = sq.astype(BF)
        lo = (sq - hi.astype(F32)).astype(BF)
        ss = _dot(hi, bd_ref[...]) + _dot(lo, bd_ref[...])
        return z * lax.rsqrt(ss * (1.0 / HEAD_DIM) + EPS) * g_ref[...]

    gl = _dot(hb, wglu_ref[...])
    glu_ref[...] = gl[:, :CONV_DIM] * jax.nn.sigmoid(gl[:, CONV_DIM:])
    q = head_norm(_dot(hb, wq_ref[...]), gq_ref)
    k = head_norm(_dot(hb, wk_ref[...]), gk_ref)
    v = _dot(hb, wv_ref[...])
    qs = q * HEAD_DIM ** -0.5
    qis = _dot(hb, wqi_ref[...]) * IDX_DIM ** -0.5
    kw = _dot(hb, wkw_ref[...])
    sgc_ref[...] = jax.nn.sigmoid(_dot(hb, wgc_ref[...]))
    sga_ref[...] = jax.nn.sigmoid(_dot(hb, wga_ref[...]))
    if per_row:
        k_ref, v_ref, kw_ref, qb_ref, qib_ref = attn_refs
        k_ref[...] = k
        v_ref[...] = v
        kw_ref[...] = kw
        qb_ref[...] = qs.astype(BF)
        qib_ref[...] = qis.astype(BF)
    else:
        kT_ref, vT_ref, kiT_ref, qTb_ref, qiTb_ref, vTb_ref, kh_ref, ki_ref, wiT_ref = attn_refs
        v_t = v.T
        kw_t = kw.T
        kT_ref[0] = k.T
        vT_ref[0] = v_t
        kiT_ref[0] = kw_t[:IDX_DIM, :]
        qTb_ref[0, 0] = qs.T.astype(BF)
        qiTb_ref[0, 0] = qis.T.astype(BF)
        vTb_ref[0, 0] = v_t.astype(BF)
        for hd in range(N_HEADS):
            kh_ref[0, hd, 0] = k[:, hd * HEAD_DIM:(hd + 1) * HEAD_DIM].astype(BF)
        ki_ref[0, 0] = kw[:, :IDX_DIM].astype(BF)
        wiT_ref[0] = kw_t[IDX_DIM:IDX_DIM + IDX_HEADS, :]


def _inproj(x, mod, per_row, rows_per_seq, tm, wts):
    t = x.shape[0]
    row = lambda n: pl.BlockSpec((tm, n), lambda i: (i, 0))
    rows = lambda n, dt: jax.ShapeDtypeStruct((t, n), dt)
    out_specs = [row(CONV_DIM), row(D_MODEL), row(D_MODEL)]
    out_shape = [rows(CONV_DIM, F32), rows(D_MODEL, F32), rows(D_MODEL, F32)]
    if per_row:
        out_specs += [row(ATTN_DIM), row(ATTN_DIM), row(LANES), row(ATTN_DIM), row(ATTN_DIM)]
        out_shape += [rows(ATTN_DIM, F32), rows(ATTN_DIM, F32), rows(LANES, F32), rows(ATTN_DIM, BF), rows(ATTN_DIM, BF)]
    else:
        nq = rows_per_seq // tm
        b = t // rows_per_seq
        seq_minor = lambda n: pl.BlockSpec((1, n, tm), lambda i: (i // nq, 0, i % nq))
        minor = pl.BlockSpec((1, 1, ATTN_DIM, tm), lambda i: (i // nq, i % nq, 0, 0))
        minor_shape = jax.ShapeDtypeStruct((b, nq, ATTN_DIM, tm), BF)
        out_specs += [seq_minor(ATTN_DIM), seq_minor(ATTN_DIM), seq_minor(IDX_DIM),
                      minor, minor, minor,
                      pl.BlockSpec((1, N_HEADS, 1, tm, HEAD_DIM), lambda i: (i // nq, 0, i % nq, 0, 0)),
                      pl.BlockSpec((1, 1, tm, IDX_DIM), lambda i: (i // nq, i % nq, 0, 0)),
                      seq_minor(IDX_HEADS)]
        out_shape += [jax.ShapeDtypeStruct((b, ATTN_DIM, rows_per_seq), F32),
                      jax.ShapeDtypeStruct((b, ATTN_DIM, rows_per_seq), F32),
                      jax.ShapeDtypeStruct((b, IDX_DIM, rows_per_seq), F32),
                      minor_shape, minor_shape, minor_shape,
                      jax.ShapeDtypeStruct((b, N_HEADS, nq, tm, HEAD_DIM), BF),
                      jax.ShapeDtypeStruct((b, nq, tm, IDX_DIM), BF),
                      jax.ShapeDtypeStruct((b, IDX_HEADS, rows_per_seq), F32)]
    return pl.pallas_call(
        functools.partial(_inproj_kernel, per_row),
        grid=(t // tm,),
        in_specs=[row(D_MODEL), _mod_spec(per_row, tm, rows_per_seq)] + [_full(w.shape) for w in wts],
        out_specs=out_specs,
        out_shape=out_shape,
        compiler_params=_params(1),
        name="inproj",
    )(x, mod, *wts)


CONV_HALO = 32


def _conv_prompt_kernel(tt, glu_ref, wdw_ref, bdw_ref, y_ref, ext_ref):
    j = pl.program_id(1)

    @pl.when(j == 0)
    def _():
        ext_ref[0:CONV_HALO, :] = jnp.zeros((CONV_HALO, CONV_DIM), F32)
        ext_ref[CONV_HALO:, :] = glu_ref[0]

    n = tt + CONV_HALO
    t0 = pl.multiple_of(j * tt, tt)
    win = ext_ref[pl.ds(t0, n), :]
    acc = jnp.zeros((tt, CONV_DIM), F32) + bdw_ref[...]
    lead = CONV_HALO - CONV_STATE
    for r in range(SUBLANES):
        rolled = win if r == 0 else pltpu.roll(win, n - r, axis=0)
        for tap in range(CONV_WIDTH):
            off = lead + tap
            if off % SUBLANES == r:
                a = off - r
                acc = acc + wdw_ref[tap:tap + 1, :] * rolled[a:a + tt]
    y_ref[0] = acc


def _conv_prompt(glu, wdw, bdw, tt):
    b, s, _ = glu.shape
    return pl.pallas_call(
        functools.partial(_conv_prompt_kernel, tt),
        grid=(b, s // tt),
        in_specs=[pl.BlockSpec((1, s, CONV_DIM), lambda i, j: (i, 0, 0)),
                  _full(wdw.shape), _full(bdw.shape)],
        out_specs=pl.BlockSpec((1, tt, CONV_DIM), lambda i, j: (i, j, 0)),
        out_shape=jax.ShapeDtypeStruct((b, s, CONV_DIM), F32),
        scratch_shapes=[pltpu.VMEM((s + CONV_HALO, CONV_DIM), F32)],
        compiler_params=_params(2),
        name="conv_prompt",
    )(glu, wdw, bdw)


def _conv_sample_kernel(st_ref, glu_ref, wdw_ref, bdw_ref, y_ref):
    acc = bdw_ref[...] + wdw_ref[CONV_STATE:CONV_STATE + 1, :] * glu_ref[...]
    for j in range(CONV_STATE):
        acc = acc + wdw_ref[j:j + 1, :] * st_ref[j]
    y_ref[...] = acc


def _conv_sample(state_t, glu, wdw, bdw):
    db = glu.shape[0]
    return pl.pallas_call(
        _conv_sample_kernel,
        grid=(1,),
        in_specs=[_full(state_t.shape), _full(glu.shape), _full(wdw.shape), _full(bdw.shape)],
        out_specs=_full((db, CONV_DIM)),
        out_shape=jax.ShapeDtypeStruct((db, CONV_DIM), F32),
        compiler_params=_params(1),
        name="conv_sample",
    )(state_t, glu, wdw, bdw)


def _rel_bucket_np(dist):
    max_exact = REL_BUCKETS // 2
    d = np.maximum(dist, 0)
    df = np.maximum(d, 1).astype(np.float32)
    large = max_exact + (np.log(df / max_exact) / math.log(REL_MAX_DIST / max_exact)
                         * (REL_BUCKETS - max_exact)).astype(np.int32)
    return np.where(d < max_exact, d, np.minimum(large, REL_BUCKETS - 1)).astype(np.int32)


def _topk_threshold(count_ge, shape, kk):
    def bit_body(bi, thr_u):
        cand_u = thr_u | jnp.left_shift(jnp.int32(1), 31 - bi)
        cnt = count_ge(cand_u ^ INT_MIN)
        return jnp.where(cnt >= kk, cand_u, thr_u)

    thr_u = lax.fori_loop(0, 32, bit_body, jnp.zeros(shape, I32))
    return thr_u ^ INT_MIN


def _attn_prompt_kernel(tq, n_toep, kk, cb_ref, qT_ref, qiT_ref, wiT_ref, kh_ref, vT_ref, ki_ref, toep_ref,
                        o_ref, keys_ref, m_ref, l_ref, acc_ref):
    tk = tq
    qt = pl.program_id(1)
    nk = qt + 1
    krow = lax.broadcasted_iota(I32, (tk, tq), 0)
    qcol = lax.broadcasted_iota(I32, (tk, tq), 1)

    def score_body(kt, carry):
        kic = ki_ref[0, kt]
        sc = jnp.zeros((tk, tq), F32)
        for h in range(IDX_HEADS):
            s = _dot(kic, qiT_ref[0, 0, h])
            sc = sc + jnp.maximum(s, 0.0) * (wiT_ref[0, h:h + 1, :] * IDX_HEADS ** -0.5)
        causal = (kt * tk + krow) <= (qt * tq + qcol)
        keys_ref[kt] = jnp.where(causal, _monotone_key(sc), INT_MIN)
        return carry

    lax.fori_loop(0, nk, score_body, 0)

    def count_ge(cand):
        def cnt_body(kt, c):
            hit = jnp.where(keys_ref[kt] >= cand, 1, 0)
            return c + jnp.sum(hit.reshape(tk // SUBLANES, SUBLANES, tq), axis=0)
        c = lax.fori_loop(0, nk, cnt_body, jnp.zeros((SUBLANES, tq), I32))
        return jnp.sum(c, axis=0, keepdims=True)

    thr = jnp.maximum(_topk_threshold(count_ge, (1, tq), kk), INT_MIN + 1)

    m_ref[...] = jnp.full(m_ref.shape, NEG, F32)
    l_ref[...] = jnp.zeros(l_ref.shape, F32)
    acc_ref[...] = jnp.zeros(acc_ref.shape, F32)

    def kv_step(kt, bias_of, thr_k):
        mask = keys_ref[kt] >= thr_k
        for h in range(N_HEADS):
            s = _dot(kh_ref[0, h, kt], qT_ref[0, 0, h]) + bias_of(h)
            m_old = m_ref[h]
            m_new = jnp.maximum(m_old, jnp.max(jnp.where(mask, s, NEG), axis=0, keepdims=True))
            p = jnp.where(mask, jnp.exp(s - m_new), 0.0)
            alpha = jnp.exp(m_old - m_new)
            l_ref[h] = alpha * l_ref[h] + jnp.sum(p, axis=0, keepdims=True)
            acc_ref[h] = alpha * acc_ref[h] + _dot(vT_ref[0, kt, h], p.astype(BF))
            m_ref[h] = m_new

    def far_body(kt, carry):
        kv_step(kt, lambda h: cb_ref[h], thr)
        return carry

    lax.fori_loop(0, jnp.maximum(nk - n_toep, 0), far_body, 0)
    for o in reversed(range(n_toep)):
        thr_o = jnp.where(qt >= o, thr, INT_MAX)
        kv_step(jnp.maximum(qt - o, 0), lambda h, o=o: toep_ref[o, h], thr_o)
    for h in range(N_HEADS):
        o_ref[0, h] = acc_ref[h] / l_ref[h]


def _attn_prompt(cb, qT, qiT, wiT, kh, vT, ki, toep, tq, kk):
    b, nq = qT.shape[:2]
    s = nq * tq
    n_toep = toep.shape[0]
    q_tile = lambda n: pl.BlockSpec((1, 1, N_HEADS, n, tq), lambda i, j: (i, j, 0, 0, 0))
    return pl.pallas_call(
        functools.partial(_attn_prompt_kernel, tq, n_toep, kk),
        grid=(b, nq),
        in_specs=[pl.BlockSpec(memory_space=pltpu.SMEM),
                  q_tile(HEAD_DIM), q_tile(IDX_DIM),
                  pl.BlockSpec((1, IDX_HEADS, tq), lambda i, j: (i, 0, j)),
                  pl.BlockSpec((1, N_HEADS, nq, tq, HEAD_DIM), lambda i, j: (i, 0, 0, 0, 0)),
                  pl.BlockSpec((1, nq, N_HEADS, HEAD_DIM, tq), lambda i, j: (i, 0, 0, 0, 0)),
                  pl.BlockSpec((1, nq, tq, IDX_DIM), lambda i, j: (i, 0, 0, 0)),
                  _full(toep.shape)],
        out_specs=pl.BlockSpec((1, N_HEADS, HEAD_DIM, tq), lambda i, j: (i, 0, 0, j)),
        out_shape=jax.ShapeDtypeStruct((b, N_HEADS, HEAD_DIM, s), F32),
        scratch_shapes=[pltpu.VMEM((nq, tq, tq), I32),
                        pltpu.VMEM((N_HEADS, 1, tq), F32), pltpu.VMEM((N_HEADS, 1, tq), F32),
                        pltpu.VMEM((N_HEADS, HEAD_DIM, tq), F32)],
        compiler_params=_params(2),
        name="attn_prompt",
    )(cb, qT, qiT, wiT, kh, vT, ki, toep)


PAGES_PER_STEP = 16


def _page_specs(block, n_pages, per_step=PAGES_PER_STEP):
    def spec(r):
        return pl.BlockSpec(block, lambda b, g, pt: (0, pt[b * n_pages + g * per_step + r]) + (0,) * (len(block) - 2))
    return [spec(r) for r in range(per_step)]


def _samp_scores_kernel(pt_ref, qi_ref, wi_ref, *refs):
    pages, o_ref = refs[:-1], refs[-1]
    qi = qi_ref[0]
    w = wi_ref[0] * IDX_HEADS ** -0.5
    rows = []
    for page in pages:
        s = _dot(qi, page[0, 0].astype(BF))
        rows.append(jnp.sum(jnp.maximum(s, 0.0) * w, axis=0, keepdims=True))
    o_ref[0] = jnp.concatenate(rows, axis=0)


def _samp_scores(page_table_flat, qi, wi, cache_idx_t, n_pages):
    db = qi.shape[0]
    grid_spec = pltpu.PrefetchScalarGridSpec(
        num_scalar_prefetch=1,
        grid=(db, 1),
        in_specs=[pl.BlockSpec((1, IDX_HEADS, IDX_DIM), lambda b, g, pt: (b, 0, 0)),
                  pl.BlockSpec((1, IDX_HEADS, 1), lambda b, g, pt: (b, 0, 0))]
                 + _page_specs((1, 1, IDX_DIM, PAGE_SIZE), n_pages, n_pages),
        out_specs=pl.BlockSpec((1, n_pages, PAGE_SIZE), lambda b, g, pt: (b, 0, 0)),
    )
    return pl.pallas_call(
        _samp_scores_kernel,
        grid_spec=grid_spec,
        out_shape=jax.ShapeDtypeStruct((db, n_pages, PAGE_SIZE), F32),
        compiler_params=_params(2),
        name="samp_scores",
    )(page_table_flat, qi, wi, *([cache_idx_t] * n_pages))


SEQ_PER_STEP = 8


def _total(x):
    return jnp.sum(jnp.sum(x, axis=0, keepdims=True), axis=1, keepdims=True)


def _ind(m):
    return jnp.where(m, 1.0, 0.0)


def _samp_select_kernel(kk, sc_ref, qi_ref, ki8_ref, kw_ref, tri_ref, lt_ref, sel_ref, nsel_ref,
                        keys_ref, thr_ref, knew_ref):
    key_new = []
    for r in range(SEQ_PER_STEP):
        prod = qi_ref[r].astype(F32) * ki8_ref[r].astype(F32)
        kwv = kw_ref[r]
        s_new = jnp.zeros((1, 1), F32)
        for h in range(IDX_HEADS):
            s = jnp.sum(prod[:, h * IDX_DIM:(h + 1) * IDX_DIM], axis=1, keepdims=True)
            s_new = s_new + jnp.maximum(s, 0.0) * (kwv[:, IDX_DIM + h:IDX_DIM + h + 1] * IDX_HEADS ** -0.5)
        key_new.append(_monotone_key(s_new))
        keys_ref[r] = _monotone_key(sc_ref[r])

    def bit_body(bi, thr_us):
        bit = jnp.left_shift(jnp.int32(1), 31 - bi)
        out = []
        for r in range(SEQ_PER_STEP):
            cand_u = thr_us[r] | bit
            cand = cand_u ^ INT_MIN
            cnt = _total(jnp.where(keys_ref[r] >= cand, 1, 0)) + jnp.where(key_new[r] >= cand, 1, 0)
            out.append(jnp.where(cnt >= kk, cand_u, thr_us[r]))
        return tuple(out)

    thr_us = lax.fori_loop(0, 32, bit_body, tuple(jnp.zeros((1, 1), I32) for _ in range(SEQ_PER_STEP)))
    for r in range(SEQ_PER_STEP):
        thr_ref[r] = jnp.broadcast_to(thr_us[r] ^ INT_MIN, (1, LANES))
        knew_ref[r] = jnp.broadcast_to(key_new[r], (1, LANES))

    def prefix(x):
        incl = _dot(x.astype(BF), tri_ref[...])
        page_tot = jnp.broadcast_to(incl[:, PAGE_SIZE - 1:PAGE_SIZE], incl.shape)
        return incl - x + _dot(lt_ref[...], page_tot.astype(BF))

    def seq_body(r, carry):
        keys = keys_ref[r]
        thr = thr_ref[r][:, :1]
        kn = knew_ref[r][:, :1]
        gt = keys > thr
        eq = keys == thr
        eq_f = _ind(eq)
        need = kk - (_total(_ind(gt)) + _ind(kn > thr))
        sel = jnp.where(gt, 1.0, jnp.where(eq, _ind(prefix(eq_f) < need), 0.0))
        new_sel = jnp.where(kn > thr, 1.0, jnp.where(kn == thr, _ind(_total(eq_f) < need), 0.0))
        sel_ref[r] = sel
        nsel_ref[r] = jnp.broadcast_to(new_sel, (1, LANES))
        return carry

    lax.fori_loop(0, SEQ_PER_STEP, seq_body, 0)


def _samp_select(scores, qib, ki8, kw, kk):
    db, n_pages, _ = scores.shape
    spb = SEQ_PER_STEP
    assert db % spb == 0
    pos = np.arange(PAGE_SIZE)
    tri = jnp.asarray(pos[:, None] <= pos[None, :], BF)
    pg = np.arange(n_pages)
    lt = jnp.asarray(pg[None, :] < pg[:, None], BF)
    per_seq = lambda n: pl.BlockSpec((spb, 1, n), lambda i: (i, 0, 0))
    return pl.pallas_call(
        functools.partial(_samp_select_kernel, kk),
        grid=(db // spb,),
        in_specs=[pl.BlockSpec((spb, n_pages, PAGE_SIZE), lambda i: (i, 0, 0)),
                  per_seq(qib.shape[2]), per_seq(ki8.shape[2]), per_seq(kw.shape[2]),
                  _full(tri.shape), _full(lt.shape)],
        out_specs=[pl.BlockSpec((spb, n_pages, PAGE_SIZE), lambda i: (i, 0, 0)), per_seq(LANES)],
        out_shape=[jax.ShapeDtypeStruct((db, n_pages, PAGE_SIZE), F32),
                   jax.ShapeDtypeStruct((db, 1, LANES), F32)],
        scratch_shapes=[pltpu.VMEM((spb, n_pages, PAGE_SIZE), I32),
                        pltpu.VMEM((spb, 1, LANES), I32), pltpu.VMEM((spb, 1, LANES), I32)],
        compiler_params=_params(1),
        name="samp_select",
    )(scores, qib, ki8, kw, tri, lt)


def _samp_attn_kernel(pt_ref, q_ref, qt_ref, sel_ref, bias_ref, bnew_ref, kn_ref, vnt_ref, *refs):
    k_pages = refs[:PAGES_PER_STEP]
    v_pages = refs[PAGES_PER_STEP:2 * PAGES_PER_STEP]
    o_ref, m_ref, l_ref, acc_ref, qb_ref = refs[2 * PAGES_PER_STEP:]
    g = pl.program_id(1)

    @pl.when(g == 0)
    def _():
        m_ref[...] = jnp.full(m_ref.shape, NEG, F32)
        l_ref[...] = jnp.zeros(l_ref.shape, F32)
        acc_ref[...] = jnp.zeros(acc_ref.shape, F32)
        for h in range(N_HEADS):
            qb_ref[h] = jnp.broadcast_to(qt_ref[0][:, h:h + 1], (HEAD_DIM, PAGE_SIZE))

    logits, masks = [], []
    for r in range(PAGES_PER_STEP):
        rows = [jnp.sum(k_pages[r][0, 0, h] * qb_ref[h], axis=0, keepdims=True) for h in range(N_HEADS)]
        mask = sel_ref[0, r:r + 1, :] > 0.5
        masks.append(mask)
        logits.append(jnp.where(mask, jnp.concatenate(rows, axis=0) + bias_ref[0, r], NEG))
    m_old = m_ref[...]
    m_new = m_old
    for s in logits:
        m_new = jnp.maximum(m_new, jnp.max(s, axis=1, keepdims=True))
    alpha = jnp.exp(m_old - m_new)
    probs = [jnp.where(mask, jnp.exp(s - m_new), 0.0) for s, mask in zip(logits, masks)]
    l = alpha * l_ref[...]
    for p in probs:
        l = l + jnp.sum(p, axis=1, keepdims=True)
    l_ref[...] = l
    m_ref[...] = m_new
    for h in range(N_HEADS):
        acc = acc_ref[h] * alpha[h:h + 1, :]
        for r in range(PAGES_PER_STEP):
            acc = acc + v_pages[r][0, 0, h] * probs[r][h:h + 1, :]
        acc_ref[h] = acc

    @pl.when(g == pl.num_programs(1) - 1)
    def _():
        s_new = jnp.sum(kn_ref[0] * q_ref[0], axis=1, keepdims=True) + bnew_ref[0]
        m_fin = jnp.maximum(m_new, s_new)
        a_fin = jnp.exp(m_new - m_fin)
        p_new = jnp.exp(s_new - m_fin)
        l_fin = a_fin * l + p_new
        cols = []
        for h in range(N_HEADS):
            past = jnp.sum(acc_ref[h], axis=1, keepdims=True) * a_fin[h:h + 1, :]
            cols.append((past + vnt_ref[0][:, h:h + 1] * p_new[h:h + 1, :]) / l_fin[h:h + 1, :])
        o_ref[0] = jnp.concatenate(cols, axis=1)


def _samp_attn(page_table_flat, q, qt, sel, bias_t, bias_new, kn, vnt, cache_k_t, cache_v_t, n_pages):
    db = q.shape[0]
    ng = n_pages // PAGES_PER_STEP
    per_seq = lambda shape: pl.BlockSpec((1,) + shape, lambda b, g, pt: (b,) + (0,) * len(shape))
    page_block = (1, 1, N_HEADS, HEAD_DIM, PAGE_SIZE)
    grid_spec = pltpu.PrefetchScalarGridSpec(
        num_scalar_prefetch=1,
        grid=(db, ng),
        in_specs=[per_seq((N_HEADS, HEAD_DIM)), per_seq((HEAD_DIM, N_HEADS)),
                  pl.BlockSpec((1, PAGES_PER_STEP, PAGE_SIZE), lambda b, g, pt: (b, g, 0)),
                  pl.BlockSpec((1, PAGES_PER_STEP, N_HEADS, PAGE_SIZE), lambda b, g, pt: (g, 0, 0, 0)),
                  per_seq((N_HEADS, 1)), per_seq((N_HEADS, HEAD_DIM)), per_seq((HEAD_DIM, N_HEADS))]
                 + _page_specs(page_block, n_pages) + _page_specs(page_block, n_pages),
        out_specs=per_seq((HEAD_DIM, N_HEADS)),
        scratch_shapes=[pltpu.VMEM((N_HEADS, 1), F32), pltpu.VMEM((N_HEADS, 1), F32),
                        pltpu.VMEM((N_HEADS, HEAD_DIM, PAGE_SIZE), F32),
                        pltpu.VMEM((N_HEADS, HEAD_DIM, PAGE_SIZE), F32)],
    )
    return pl.pallas_call(
        _samp_attn_kernel,
        grid_spec=grid_spec,
        out_shape=jax.ShapeDtypeStruct((db, HEAD_DIM, N_HEADS), F32),
        compiler_params=_params(2),
        name="samp_attn",
    )(page_table_flat, q, qt, sel, bias_t, bias_new, kn, vnt,
      *([cache_k_t] * PAGES_PER_STEP), *([cache_v_t] * PAGES_PER_STEP))


def _merge_kernel(per_row, x_ref, yc_ref, at_ref, sgc_ref, sga_ref, mod_ref, lng_ref, lnb_ref,
                  wco_ref, wao_ref, wout_ref, g2_ref, wpq_ref, kbd_ref,
                  xm_ref, h2t_ref, st_ref):
    yc = yc_ref[...]
    mu = jnp.mean(yc, axis=-1, keepdims=True)
    yd = yc - mu
    ln = yd * lax.rsqrt(jnp.mean(yd * yd, axis=-1, keepdims=True) + EPS) * lng_ref[...] + lnb_ref[...]
    act = ln * jax.nn.sigmoid(ln)
    conv_out = _dot(act.astype(BF), wco_ref[...])
    attn_out = _dot(at_ref[0].T.astype(BF), wao_ref[...])
    merged = sgc_ref[...] * conv_out + sga_ref[...] * attn_out
    xm = x_ref[...] + _mod(mod_ref, 2, per_row) * _dot(merged.astype(BF), wout_ref[...])
    xm_ref[...] = xm
    rs = lax.rsqrt(jnp.mean(xm * xm, axis=-1, keepdims=True) + EPS)
    h2 = (xm * rs) * g2_ref[...] * (1.0 + _mod(mod_ref, 4, per_row)) + _mod(mod_ref, 3, per_row)
    h2t_ref[...] = pltpu.bitcast(h2.T.astype(BF), I32)
    pq = _dot(h2.astype(BF), wpq_ref[...])
    st_ref[...] = _dot_nt(kbd_ref[...], pq.astype(BF))


def _merge(x, yc, at, sgc, sga, mod, per_row, rows_per_seq, tm, wts):
    t = x.shape[0]
    row = lambda n: pl.BlockSpec((tm, n), lambda i: (i, 0))
    col = lambda n: pl.BlockSpec((n, tm), lambda i: (0, i))
    n_scores = PEER_HEADS * 2 * PEER_KEYS
    return pl.pallas_call(
        functools.partial(_merge_kernel, per_row),
        grid=(t // tm,),
        in_specs=[row(D_MODEL), row(CONV_DIM),
                  pl.BlockSpec((1, ATTN_DIM, tm), lambda i: (i // (at.shape[2] // tm), 0, i % (at.shape[2] // tm))),
                  row(D_MODEL), row(D_MODEL),
                  _mod_spec(per_row, tm, rows_per_seq)] + [_full(w.shape) for w in wts],
        out_specs=[row(D_MODEL), col(D_MODEL // 2), col(n_scores)],
        out_shape=[jax.ShapeDtypeStruct((t, D_MODEL), F32),
                   jax.ShapeDtypeStruct((D_MODEL // 2, t), I32),
                   jax.ShapeDtypeStruct((n_scores, t), F32)],
        compiler_params=_params(1),
        name="merge",
    )(x, yc, at, sgc, sga, mod, *wts)


N_EXTRACT = PEER_TOPK + 1


def _top_extract(cur, n):
    outs = []
    for _ in range(n):
        mx = jnp.max(cur, axis=0, keepdims=True)
        outs.append(mx)
        cur = jnp.where(cur >= mx, -jnp.inf, cur)
    return outs


RANK_NONE = float(PEER_KEYS - 1)


def _peer_select_kernel(lb, st_ref, r1_ref, e1_ref, n_ref, al_ref):
    tl = st_ref.shape[1]

    def head_body(h, carry):
        base = pl.multiple_of(h * 2 * PEER_KEYS, 2 * PEER_KEYS)
        s0 = st_ref[pl.ds(base, PEER_KEYS), :]
        s1 = st_ref[pl.ds(base + PEER_KEYS, PEER_KEYS), :]
        a0 = _top_extract(s0, N_EXTRACT)
        a1 = []
        cur = s1
        rank1 = jnp.full(s1.shape, RANK_NONE, F32)
        for k in range(N_EXTRACT):
            mx = jnp.max(cur, axis=0, keepdims=True)
            a1.append(mx)
            hit = cur >= mx
            if k < PEER_TOPK:
                rank1 = jnp.where(hit, float(k), rank1)
            cur = jnp.where(hit, -jnp.inf, cur)
        pad = [jnp.full((1, tl), -jnp.inf, F32)] * (3 * SUBLANES - N_EXTRACT)
        a1_all = jnp.concatenate(a1 + pad, axis=0)
        cands = [a0[0] + a1_all] + [a0[k] + a1_all[:SUBLANES] for k in range(1, N_EXTRACT)]
        top = _top_extract(jnp.concatenate(cands, axis=0), N_EXTRACT)
        thr = 0.5 * (top[PEER_TOPK - 1] + top[PEER_TOPK])
        z = jnp.zeros_like(thr)
        for k in range(PEER_TOPK):
            z = z + jnp.exp(top[k] - top[0])
        tau = thr - s0
        cnt = jnp.zeros(s0.shape, F32)
        for l in range(PEER_TOPK):
            cnt = cnt + jnp.where(a1[l] >= tau, 1.0, 0.0)
        e1 = jnp.exp(s1 - a1[0])
        al = jnp.exp(s0 - a0[0]) / z
        for tb in range(tl // lb):
            ts = slice(tb * lb, (tb + 1) * lb)
            hi_bits = lambda v: lax.bitcast_convert_type(v[:, ts].astype(BF).astype(F32), I32)
            both = lambda b: b | lax.shift_right_logical(b, 16)
            half = PEER_KEYS // 2
            pair = lambda b: lax.shift_right_logical(b[:half], 16) | b[half:]
            r1_ref[h, tb] = both(hi_bits(rank1))
            e1_ref[h, tb] = both(hi_bits(e1))
            n_ref[h, tb] = pair(hi_bits(cnt))
            al_ref[h, tb] = pair(hi_bits(al))
        return carry

    lax.fori_loop(0, PEER_HEADS, head_body, 0)


def _peer_select(st, tl):
    t = st.shape[1]
    lb = min(LANES, t)
    spec = lambda rows: pl.BlockSpec((PEER_HEADS, tl // lb, rows, lb), lambda i: (0, i, 0, 0))
    shp = lambda rows: jax.ShapeDtypeStruct((PEER_HEADS, t // lb, rows, lb), I32)
    per_j, per_i = PEER_KEYS, PEER_KEYS // 2
    return pl.pallas_call(
        functools.partial(_peer_select_kernel, lb),
        grid=(t // tl,),
        in_specs=[pl.BlockSpec((st.shape[0], tl), lambda i: (0, i))],
        out_specs=[spec(per_j), spec(per_j), spec(per_i), spec(per_i)],
        out_shape=[shp(per_j), shp(per_j), shp(per_i), shp(per_i)],
        compiler_params=_params(1),
        name="peer_select",
    )(st)


PEER_PAIRS_PER_STEP = SUBLANES


def _peer_main_kernel(per_row, h2t_ref, ulo_ref, uhi_ref, vtlo_ref, vthi_ref, r1_ref, e1_ref, n_ref, al_ref,
                      xm_ref, mod_ref, o_ref, acc_ref, a_ref, g_ref):
    j = pl.program_id(1)
    tm = h2t_ref.shape[1]
    n_tb, _, lb = r1_ref.shape[1:]

    @pl.when(j == 0)
    def _():
        acc_ref[...] = jnp.zeros(acc_ref.shape, F32)

    i0 = pl.multiple_of(j * PEER_PAIRS_PER_STEP, PEER_PAIRS_PER_STEP)
    h2t = pltpu.bitcast(h2t_ref[...], BF)
    unit = 2 * PEER_KEYS
    n_units = PEER_PAIRS_PER_STEP // 2
    jn = PEER_KEYS // 2
    u_refs = (ulo_ref, uhi_ref)
    vt_refs = (vtlo_ref, vthi_ref)

    def unpacked(ref, lo, n):
        return pltpu.bitcast(ref[lo:lo + n], BF)

    def activations(un):
        for c in range(2):
            a_ref[un % 2, c] = _dot(unpacked(u_refs[c], un * unit // 2, unit // 2), h2t)

    activations(0)
    for un in range(n_units):
        rows = slice(un * unit, (un + 1) * unit)
        if un + 1 < n_units:
            activations(un + 1)
        for tb in range(n_tb):
            ts = slice(tb * lb, (tb + 1) * lb)
            for jb in range(PEER_KEYS // jn):
                js = slice(jb * jn, (jb + 1) * jn)
                w = [jnp.zeros((2 * jn, lb), BF), jnp.zeros((2 * jn, lb), BF)]
                for h in range(PEER_HEADS):
                    cnt = n_ref[h, tb, pl.ds(i0, PEER_PAIRS_PER_STEP), :]
                    al = al_ref[h, tb, pl.ds(i0, PEER_PAIRS_PER_STEP), :]
                    r1 = pltpu.bitcast(r1_ref[h, tb, js, :], BF)
                    e1 = pltpu.bitcast(e1_ref[h, tb, js, :], BF)
                    for q in range(2):
                        pp = 2 * un + q
                        cnt_b = pltpu.bitcast(jnp.broadcast_to(cnt[pp:pp + 1], (jn, lb)), BF)
                        al_b = pltpu.bitcast(jnp.broadcast_to(al[pp:pp + 1], (jn, lb)), BF)
                        w[q] = w[q] + jnp.where(r1 < cnt_b, e1 * al_b, 0)
                for q in range(2):
                    wi = pltpu.bitcast(w[q], I32)
                    halves = (lax.bitcast_convert_type(lax.shift_left(wi, 16), F32),
                              lax.bitcast_convert_type(wi & jnp.int32(-65536), F32))
                    for c in range(2):
                        lo = q * PEER_KEYS + jb * jn
                        a = a_ref[un % 2, c, lo:lo + jn, ts]
                        gelu = 0.5 * a * (1.0 + lax.erf(a * (2.0 ** -0.5)))
                        g_ref[c, un * unit + lo:un * unit + lo + jn, ts] = (halves[c] * gelu).astype(BF)
        acc_ref[...] += (_dot(pltpu.bitcast(vtlo_ref[:, rows], BF), g_ref[0, rows, :])
                         + _dot(pltpu.bitcast(vthi_ref[:, rows], BF), g_ref[1, rows, :]))

    @pl.when(j == pl.num_programs(1) - 1)
    def _():
        o_ref[...] = xm_ref[...] + _mod(mod_ref, 5, per_row) * acc_ref[...].T


def _peer_main(h2t, u_pk, vt_pk, r1, e1, cnt, al, xm, mod, per_row, rows_per_seq, tm):
    t = xm.shape[0]
    ec = PEER_PAIRS_PER_STEP * PEER_KEYS
    n_steps = u_pk.shape[0] * 2 // (2 * ec)
    lb = r1.shape[3]
    sel = lambda rows: pl.BlockSpec((PEER_HEADS, tm // lb, rows, lb), lambda i, j: (0, i, 0, 0))
    if per_row:
        mod_spec = pl.BlockSpec((6, tm, D_MODEL), lambda i, j: (0, i, 0))
    else:
        tiles_per_seq = rows_per_seq // tm
        mod_spec = pl.BlockSpec((1, 6, D_MODEL), lambda i, j: (i // tiles_per_seq, 0, 0))
    return pl.pallas_call(
        functools.partial(_peer_main_kernel, per_row),
        grid=(t // tm, n_steps),
        in_specs=[pl.BlockSpec((D_MODEL // 2, tm), lambda i, j: (0, i)),
                  pl.BlockSpec((ec // 2, D_MODEL), lambda i, j: (j, 0)),
                  pl.BlockSpec((ec // 2, D_MODEL), lambda i, j: (j + n_steps, 0)),
                  pl.BlockSpec((D_MODEL // 2, ec), lambda i, j: (0, j)),
                  pl.BlockSpec((D_MODEL // 2, ec), lambda i, j: (0, j + n_steps)),
                  sel(PEER_KEYS), sel(PEER_KEYS), sel(PEER_KEYS // 2), sel(PEER_KEYS // 2),
                  pl.BlockSpec((tm, D_MODEL), lambda i, j: (i, 0)),
                  mod_spec],
        out_specs=pl.BlockSpec((tm, D_MODEL), lambda i, j: (i, 0)),
        out_shape=jax.ShapeDtypeStruct((t, D_MODEL), F32),
        scratch_shapes=[pltpu.VMEM((D_MODEL, tm), F32), pltpu.VMEM((2, 2, 2 * PEER_KEYS, tm), F32),
                        pltpu.VMEM((2, ec, tm), BF)],
        compiler_params=_params(2),
        name="peer_main",
    )(h2t, u_pk, u_pk, vt_pk, vt_pk, r1, e1, cnt, al, xm, mod)


PACK_ROWS = 512


def _peer_pack_kernel(u_ref, v_ref, u_pk_ref, vt_pk_ref):
    u_pk_ref[...] = pltpu.bitcast(u_ref[...].astype(BF), I32)
    vt_pk_ref[...] = pltpu.bitcast(v_ref[...].T.astype(BF), I32)


def _peer_pack(u, v):
    n_e = u.shape[0]
    return pl.pallas_call(
        _peer_pack_kernel,
        grid=(n_e // PACK_ROWS,),
        in_specs=[pl.BlockSpec((PACK_ROWS, D_MODEL), lambda i: (i, 0))] * 2,
        out_specs=[pl.BlockSpec((PACK_ROWS // 2, D_MODEL), lambda i: (i, 0)),
                   pl.BlockSpec((D_MODEL // 2, PACK_ROWS), lambda i: (0, i))],
        out_shape=[jax.ShapeDtypeStruct((n_e // 2, D_MODEL), I32),
                   jax.ShapeDtypeStruct((D_MODEL // 2, n_e), I32)],
        compiler_params=_params(1),
        name="peer_pack",
    )(u, v)


def _tile(n, pref):
    return pref if n % pref == 0 else n


def kernel(x_prompt, x_sample, cache_k, cache_v, cache_idx_k, state_conv, page_table, c_prompt, c_sample,
           w_ada, b_ada, norm1_g, w_in, q_norm_g, k_norm_g, rel_bias, w_dw, b_dw, conv_ln_g, conv_ln_b,
           w_conv_o, w_attn_o, w_out, norm2_g, w_peer_q, peer_keys, peer_u, peer_v):
    b, s, _ = x_prompt.shape
    db, ds, _ = x_sample.shape
    n_pages = page_table.shape[1]
    past = n_pages * PAGE_SIZE
    assert w_ada.shape[0] == 1, "one layer"
    assert ds == 1, "one new token per sample sequence"
    assert n_pages % PAGES_PER_STEP == 0
    tp = b * s

    w = w_in[0]
    pts = np.cumsum([0, 2 * CONV_DIM, ATTN_DIM, ATTN_DIM, ATTN_DIM, IDX_HEADS * IDX_DIM, IDX_DIM, IDX_HEADS,
                     D_MODEL, D_MODEL])
    seg = lambda i: w[:, pts[i]:pts[i + 1]].astype(BF)
    w_kw = jnp.pad(w[:, pts[5]:pts[7]], ((0, 0), (0, LANES - IDX_DIM - IDX_HEADS))).astype(BF)
    head_of = np.arange(ATTN_DIM) // HEAD_DIM
    bd = jnp.asarray(head_of[:, None] == head_of[None, :], BF)
    row = lambda v: v.reshape(1, -1)
    in_wts = [row(norm1_g[0]), seg(0), seg(1), seg(2), seg(3), seg(4), w_kw, seg(7), seg(8),
              row(jnp.tile(q_norm_g[0], N_HEADS)), row(jnp.tile(k_norm_g[0], N_HEADS)), bd]
    n_half = PEER_HEADS * 2
    kb16 = peer_keys[0].reshape(n_half, PEER_KEYS, PEER_HALF)
    kbd_t = (jnp.eye(n_half, dtype=F32)[:, None, :, None] * kb16[:, :, None, :]).reshape(
        n_half * PEER_KEYS, n_half * PEER_HALF).astype(BF)
    merge_wts = [row(conv_ln_g[0]), row(conv_ln_b[0]), w_conv_o[0].astype(BF), w_attn_o[0].astype(BF),
                 w_out[0].astype(BF), row(norm2_g[0]), w_peer_q[0].astype(BF), kbd_t]
    u_pk, vt_pk = _peer_pack(peer_u[0], peer_v[0])

    mod = _ada(jnp.concatenate([c_prompt, c_sample], axis=0), w_ada[0].astype(BF), row(b_ada[0]))
    mod_p = mod[:b].reshape(b, 6, D_MODEL)
    mod_s = mod[b:].reshape(db, 6, D_MODEL).transpose(1, 0, 2)

    tq = _tile(s, 256)
    nq = s // tq
    glu_p, sgc_p, sga_p, kT_p, vT32_p, kiT_p, qT_p, qiT_p, vT_p, kh_p, ki_p, wiT_p = _inproj(
        x_prompt.reshape(tp, D_MODEL), mod_p, False, s, tq, in_wts)
    glu_s, sgc_s, sga_s, k_s, v_s, kw_s, qb_s, qib_s = _inproj(
        x_sample.reshape(db, D_MODEL), mod_s, True, 1, db, in_wts)

    yc_p = _conv_prompt(glu_p.reshape(b, s, CONV_DIM), w_dw[0], row(b_dw[0]), _tile(s, 256)).reshape(tp, CONV_DIM)
    yc_s = _conv_sample(state_conv[0].transpose(1, 0, 2), glu_s, w_dw[0], row(b_dw[0]))

    bucket =_rel_bucket_np(np.arange(max(s, past + 1)))
    n_toep = nq
    while n_toep > 0 and len(set(bucket[max((n_toep - 1) * tq - (tq - 1), 0):n_toep * tq])) == 1 \
            and bucket[(n_toep - 1) * tq] == bucket[s - 1]:
        n_toep -= 1
    n_toep = max(n_toep, 1)

    def bias_of_bucket(bkt):
        onehot = jnp.asarray(np.eye(REL_BUCKETS, dtype=np.float32)[bkt.reshape(-1)])
        return jnp.dot(onehot, rel_bias, precision=lax.Precision.HIGHEST).reshape(bkt.shape + (N_HEADS,))

    width = 2 * tq + 1
    dist = (np.arange(n_toep)[:, None] - 1) * tq + np.arange(width)[None, :]
    table = bias_of_bucket(_rel_bucket_np(dist)).transpose(0, 2, 1)
    skew = jnp.tile(table, (1, 1, tq))[:, :, :tq * (width - 1)].reshape(n_toep, N_HEADS, tq, width - 1)
    toep = skew[:, :, :, tq:]
    cb = rel_bias[int(bucket[s - 1])]

    kk_p = min(IDX_TOPK_MAX, s // 4)
    split_heads = lambda a: a.reshape(b, nq, N_HEADS, -1, tq)
    at_p = _attn_prompt(cb, split_heads(qT_p), split_heads(qiT_p), wiT_p, kh_p, split_heads(vT_p), ki_p,
                        toep, tq, kk_p)
    at_p = at_p.reshape(b, ATTN_DIM, s)

    kk_s = min(IDX_TOPK_MAX, (past + ds) // 4)
    pt_flat = page_table.reshape(-1)
    sc = _samp_scores(pt_flat, qib_s.reshape(db, IDX_HEADS, IDX_DIM),
                      kw_s[:, IDX_DIM:IDX_DIM + IDX_HEADS].reshape(db, IDX_HEADS, 1),
                      cache_idx_k.transpose(0, 1, 3, 2), n_pages)
    ki8 = jnp.tile(kw_s[:, :IDX_DIM].astype(BF), (1, IDX_HEADS))
    sel_s, new_sel = _samp_select(sc.reshape(db, n_pages, PAGE_SIZE), qib_s[:, None, :], ki8[:, None, :],
                                  kw_s[:, None, :], kk_s)
    pos = np.arange(past).reshape(n_pages // PAGES_PER_STEP, PAGES_PER_STEP, PAGE_SIZE)
    bias_t = bias_of_bucket(_rel_bucket_np(past - pos)).transpose(0, 1, 3, 2)
    bias_new = jnp.where(new_sel[:, 0, :1] > 0.0, rel_bias[int(_rel_bucket_np(np.zeros((), np.int64)))][None], NEG)
    heads = lambda a: a.reshape(db, N_HEADS, HEAD_DIM)
    q_s = heads(qb_s.astype(F32))
    at_s = _samp_attn(pt_flat, q_s, q_s.transpose(0, 2, 1), sel_s, bias_t, bias_new.reshape(db, N_HEADS, 1),
                      heads(k_s), heads(v_s).transpose(0, 2, 1),
                      cache_k.transpose(0, 1, 3, 4, 2), cache_v.transpose(0, 1, 3, 4, 2), n_pages)
    at_s = at_s.transpose(2, 1, 0).reshape(1, ATTN_DIM, db)

    def tail(x, yc, at, sgc, sga, mod_g, per_row, rows_per_seq, tm_merge, tl_sel, tm_peer):
        xm, h2t, st = _merge(x, yc, at, sgc, sga, mod_g, per_row, rows_per_seq, tm_merge, merge_wts)
        s1, e1, tau, al = _peer_select(st, tl_sel)
        return _peer_main(h2t, u_pk, vt_pk, s1, e1, tau, al, xm, mod_g, per_row, rows_per_seq, tm_peer)

    y_p = tail(x_prompt.reshape(tp, D_MODEL), yc_p, at_p, sgc_p, sga_p, mod_p, False, s,
               _tile(s, 256), _tile(tp, 256), _tile(s, 512))
    y_s = tail(x_sample.reshape(db, D_MODEL), yc_s, at_s, sgc_s, sga_s, mod_s, True, 1, db, db, db)

    glu_p3 = glu_p.reshape(b, s, CONV_DIM)
    conv_s = jnp.concatenate([state_conv[0], glu_s[:, None, :]], axis=1)[:, -CONV_STATE:]
    seq_major = lambda a: a.reshape(b, N_HEADS, HEAD_DIM, s).transpose(0, 3, 1, 2)[None]
    return (y_p.reshape(b, s, D_MODEL), y_s.reshape(db, ds, D_MODEL),
            seq_major(kT_p), seq_major(vT32_p),
            kiT_p.transpose(0, 2, 1)[None], glu_p3[None, :, -CONV_STATE:],
            k_s.reshape(1, db, ds, N_HEADS, HEAD_DIM), v_s.reshape(1, db, ds, N_HEADS, HEAD_DIM),
            kw_s[:, :IDX_DIM].reshape(1, db, ds, IDX_DIM), conv_s[None])
```

```python
import functools
import math

import numpy as np
import jax
import jax.numpy as jnp
from jax import lax
from jax.experimental import pallas as pl
from jax.experimental.pallas import tpu as pltpu

F32 = jnp.float32
BF = jnp.bfloat16
I32 = jnp.int32

D_MODEL = 1024
N_HEADS = 8
HEAD_DIM = 64
ATTN_DIM = N_HEADS * HEAD_DIM
IDX_HEADS = 8
IDX_DIM = 64
IDX_TOPK_MAX = 256
REL_BUCKETS = 32
REL_MAX_DIST = 128
CONV_DIM = 512
CONV_WIDTH = 31
CONV_STATE = CONV_WIDTH - 1
PEER_HEADS = 8
PEER_KEYS = 128
PEER_HALF = 64
PEER_TOPK = 16
PAGE_SIZE = 128
EPS = 1e-6

NEG = -1e30
INT_MIN = -(2 ** 31)
INT_MAX = 2 ** 31 - 1
LANES = 128
SUBLANES = 8
VMEM_LIMIT = 52 * 1024 * 1024


def _params(n_axes):
    return pltpu.CompilerParams(dimension_semantics=("arbitrary",) * n_axes,
                                vmem_limit_bytes=VMEM_LIMIT)


def _full(shape):
    zeros = (0,) * len(shape)
    return pl.BlockSpec(shape, lambda *_: zeros)


def _mod(mod_ref, k, per_row):
    return mod_ref[k] if per_row else mod_ref[0, k:k + 1, :]


def _mod_spec(per_row, tm, rows_per_seq):
    if per_row:
        return pl.BlockSpec((6, tm, D_MODEL), lambda i: (0, i, 0))
    tiles_per_seq = rows_per_seq // tm
    return pl.BlockSpec((1, 6, D_MODEL), lambda i: (i // tiles_per_seq, 0, 0))


def _dot(a, b):
    return jnp.dot(a, b, preferred_element_type=F32)


def _dot_nt(a, b):
    return lax.dot_general(a, b, (((1,), (1,)), ((), ())), preferred_element_type=F32)


def _monotone_key(x):
    bits = lax.bitcast_convert_type(x, I32)
    return bits ^ ((bits >> 31) & INT_MAX)


def _ada_kernel(c_ref, w_ref, b_ref, o_ref):
    c = c_ref[...]
    a = (c * jax.nn.sigmoid(c)).astype(BF)
    o_ref[...] = _dot(a, w_ref[...]) + b_ref[...]


def _ada(c, w_bf, b):
    rows = c.shape[0]
    n = w_bf.shape[1]
    nb = 1536
    return pl.pallas_call(
        _ada_kernel,
        grid=(n // nb,),
        in_specs=[_full((rows, D_MODEL)),
                  pl.BlockSpec((D_MODEL, nb), lambda j: (0, j)),
                  pl.BlockSpec((1, nb), lambda j: (0, j))],
        out_specs=pl.BlockSpec((rows, nb), lambda j: (0, j)),
        out_shape=jax.ShapeDtypeStruct((rows, n), F32),
        compiler_params=_params(1),
        name="ada",
    )(c, w_bf, b)


def _inproj_kernel(per_row, x_ref, mod_ref, g1_ref, wglu_ref, wq_ref, wk_ref, wv_ref, wqi_ref,
                   wkw_ref, wgc_ref, wga_ref, gq_ref, gk_ref, bd_ref,
                   glu_ref, sgc_ref, sga_ref, *attn_refs):
    x = x_ref[...]
    shift = _mod(mod_ref, 0, per_row)
    scale = _mod(mod_ref, 1, per_row)
    rs = lax.rsqrt(jnp.mean(x * x, axis=-1, keepdims=True) + EPS)
    h = (x * rs) * g1_ref[...] * (1.0 + scale) + shift
    hb = h.astype(BF)

    def head_norm(z, g_ref):
        sq = z * z
        hi = sq.astype(BF)
        lo = (sq - hi.astype(F32)).astype(BF)
        ss = _dot(hi, bd_ref[...]) + _dot(lo, bd_ref[...])
        return z * lax.rsqrt(ss * (1.0 / HEAD_DIM) + EPS) * g_ref[...]

    gl = _dot(hb, wglu_ref[...])
    glu_ref[...] = gl[:, :CONV_DIM] * jax.nn.sigmoid(gl[:, CONV_DIM:])
    q = head_norm(_dot(hb, wq_ref[...]), gq_ref)
    k = head_norm(_dot(hb, wk_ref[...]), gk_ref)
    v = _dot(hb, wv_ref[...])
    qs = q * HEAD_DIM ** -0.5
    qis = _dot(hb, wqi_ref[...]) * IDX_DIM ** -0.5
    kw = _dot(hb, wkw_ref[...])
    sgc_ref[...] = jax.nn.sigmoid(_dot(hb, wgc_ref[...]))
    sga_ref[...] = jax.nn.sigmoid(_dot(hb, wga_ref[...]))
    if per_row:
        k_ref, v_ref, kw_ref, qb_ref, qib_ref = attn_refs
        k_ref[...] = k
        v_ref[...] = v
        kw_ref[...] = kw
        qb_ref[...] = qs.astype(BF)
        qib_ref[...] = qis.astype(BF)
    else:
        kT_ref, vT_ref, kiT_ref, qTb_ref, qiTb_ref, vTb_ref, kh_ref, ki_ref, wiT_ref = attn_refs
        v_t = v.T
        kw_t = kw.T
        kT_ref[0] = k.T
        vT_ref[0] = v_t
        kiT_ref[0] = kw_t[:IDX_DIM, :]
        qTb_ref[0, 0] = qs.T.astype(BF)
        qiTb_ref[0, 0] = qis.T.astype(BF)
        vTb_ref[0, 0] = v_t.astype(BF)
        for hd in range(N_HEADS):
            kh_ref[0, hd, 0] = k[:, hd * HEAD_DIM:(hd + 1) * HEAD_DIM].astype(BF)
        ki_ref[0, 0] = kw[:, :IDX_DIM].astype(BF)
        wiT_ref[0] = kw_t[IDX_DIM:IDX_DIM + IDX_HEADS, :]


def _inproj(x, mod, per_row, rows_per_seq, tm, wts):
    t = x.shape[0]
    row = lambda n: pl.BlockSpec((tm, n), lambda i: (i, 0))
    rows = lambda n, dt: jax.ShapeDtypeStruct((t, n), dt)
    out_specs = [row(CONV_DIM), row(D_MODEL), row(D_MODEL)]
    out_shape = [rows(CONV_DIM, F32), rows(D_MODEL, F32), rows(D_MODEL, F32)]
    if per_row:
        out_specs += [row(ATTN_DIM), row(ATTN_DIM), row(LANES), row(ATTN_DIM), row(ATTN_DIM)]
        out_shape += [rows(ATTN_DIM, F32), rows(ATTN_DIM, F32), rows(LANES, F32), rows(ATTN_DIM, BF), rows(ATTN_DIM, BF)]
    else:
        nq = rows_per_seq // tm
        b = t // rows_per_seq
        seq_minor = lambda n: pl.BlockSpec((1, n, tm), lambda i: (i // nq, 0, i % nq))
        minor = pl.BlockSpec((1, 1, ATTN_DIM, tm), lambda i: (i // nq, i % nq, 0, 0))
        minor_shape = jax.ShapeDtypeStruct((b, nq, ATTN_DIM, tm), BF)
        out_specs += [seq_minor(ATTN_DIM), seq_minor(ATTN_DIM), seq_minor(IDX_DIM),
                      minor, minor, minor,
                      pl.BlockSpec((1, N_HEADS, 1, tm, HEAD_DIM), lambda i: (i // nq, 0, i % nq, 0, 0)),
                      pl.BlockSpec((1, 1, tm, IDX_DIM), lambda i: (i // nq, i % nq, 0, 0)),
                      seq_minor(IDX_HEADS)]
        out_shape += [jax.ShapeDtypeStruct((b, ATTN_DIM, rows_per_seq), F32),
                      jax.ShapeDtypeStruct((b, ATTN_DIM, rows_per_seq), F32),
                      jax.ShapeDtypeStruct((b, IDX_DIM, rows_per_seq), F32),
                      minor_shape, minor_shape, minor_shape,
                      jax.ShapeDtypeStruct((b, N_HEADS, nq, tm, HEAD_DIM), BF),
                      jax.ShapeDtypeStruct((b, nq, tm, IDX_DIM), BF),
                      jax.ShapeDtypeStruct((b, IDX_HEADS, rows_per_seq), F32)]
    return pl.pallas_call(
        functools.partial(_inproj_kernel, per_row),
        grid=(t // tm,),
        in_specs=[row(D_MODEL), _mod_spec(per_row, tm, rows_per_seq)] + [_full(w.shape) for w in wts],
        out_specs=out_specs,
        out_shape=out_shape,
        compiler_params=_params(1),
        name="inproj",
    )(x, mod, *wts)


CONV_HALO = 32


def _conv_prompt_kernel(tt, glu_ref, wdw_ref, bdw_ref, y_ref, ext_ref):
    j = pl.program_id(1)

    @pl.when(j == 0)
    def _():
        ext_ref[0:CONV_HALO, :] = jnp.zeros((CONV_HALO, CONV_DIM), F32)
        ext_ref[CONV_HALO:, :] = glu_ref[0]

    n = tt + CONV_HALO
    t0 = pl.multiple_of(j * tt, tt)
    win = ext_ref[pl.ds(t0, n), :]
    acc = jnp.zeros((tt, CONV_DIM), F32) + bdw_ref[...]
    lead = CONV_HALO - CONV_STATE
    for r in range(SUBLANES):
        rolled = win if r == 0 else pltpu.roll(win, n - r, axis=0)
        for tap in range(CONV_WIDTH):
            off = lead + tap
            if off % SUBLANES == r:
                a = off - r
                acc = acc + wdw_ref[tap:tap + 1, :] * rolled[a:a + tt]
    y_ref[0] = acc


def _conv_prompt(glu, wdw, bdw, tt):
    b, s, _ = glu.shape
    return pl.pallas_call(
        functools.partial(_conv_prompt_kernel, tt),
        grid=(b, s // tt),
        in_specs=[pl.BlockSpec((1, s, CONV_DIM), lambda i, j: (i, 0, 0)),
                  _full(wdw.shape), _full(bdw.shape)],
        out_specs=pl.BlockSpec((1, tt, CONV_DIM), lambda i, j: (i, j, 0)),
        out_shape=jax.ShapeDtypeStruct((b, s, CONV_DIM), F32),
        scratch_shapes=[pltpu.VMEM((s + CONV_HALO, CONV_DIM), F32)],
        compiler_params=_params(2),
        name="conv_prompt",
    )(glu, wdw, bdw)


def _conv_sample_kernel(st_ref, glu_ref, wdw_ref, bdw_ref, y_ref):
    acc = bdw_ref[...] + wdw_ref[CONV_STATE:CONV_STATE + 1, :] * glu_ref[...]
    for j in range(CONV_STATE):
        acc = acc + wdw_ref[j:j + 1, :] * st_ref[j]
    y_ref[...] = acc


def _conv_sample(state_t, glu, wdw, bdw):
    db = glu.shape[0]
    return pl.pallas_call(
        _conv_sample_kernel,
        grid=(1,),
        in_specs=[_full(state_t.shape), _full(glu.shape), _full(wdw.shape), _full(bdw.shape)],
        out_specs=_full((db, CONV_DIM)),
        out_shape=jax.ShapeDtypeStruct((db, CONV_DIM), F32),
        compiler_params=_params(1),
        name="conv_sample",
    )(state_t, glu, wdw, bdw)


def _rel_bucket_np(dist):
    max_exact = REL_BUCKETS // 2
    d = np.maximum(dist, 0)
    df = np.maximum(d, 1).astype(np.float32)
    large = max_exact + (np.log(df / max_exact) / math.log(REL_MAX_DIST / max_exact)
                         * (REL_BUCKETS - max_exact)).astype(np.int32)
    return np.where(d < max_exact, d, np.minimum(large, REL_BUCKETS - 1)).astype(np.int32)


def _topk_threshold(count_ge, shape, kk):
    def bit_body(bi, thr_u):
        cand_u = thr_u | jnp.left_shift(jnp.int32(1), 31 - bi)
        cnt = count_ge(cand_u ^ INT_MIN)
        return jnp.where(cnt >= kk, cand_u, thr_u)

    thr_u = lax.fori_loop(0, 32, bit_body, jnp.zeros(shape, I32))
    return thr_u ^ INT_MIN


def _attn_prompt_kernel(tq, n_toep, kk, cb_ref, qT_ref, qiT_ref, wiT_ref, kh_ref, vT_ref, ki_ref, toep_ref,
                        o_ref, keys_ref, m_ref, l_ref, acc_ref):
    tk = tq
    qt = pl.program_id(1)
    nk = qt + 1
    krow = lax.broadcasted_iota(I32, (tk, tq), 0)
    qcol = lax.broadcasted_iota(I32, (tk, tq), 1)

    def score_body(kt, carry):
        kic = ki_ref[0, kt]
        sc = jnp.zeros((tk, tq), F32)
        for h in range(IDX_HEADS):
            s = _dot(kic, qiT_ref[0, 0, h])
            sc = sc + jnp.maximum(s, 0.0) * (wiT_ref[0, h:h + 1, :] * IDX_HEADS ** -0.5)
        causal = (kt * tk + krow) <= (qt * tq + qcol)
        keys_ref[kt] = jnp.where(causal, _monotone_key(sc), INT_MIN)
        return carry

    lax.fori_loop(0, nk, score_body, 0)

    def count_ge(cand):
        def cnt_body(kt, c):
            hit = jnp.where(keys_ref[kt] >= cand, 1, 0)
            return c + jnp.sum(hit.reshape(tk // SUBLANES, SUBLANES, tq), axis=0)
        c = lax.fori_loop(0, nk, cnt_body, jnp.zeros((SUBLANES, tq), I32))
        return jnp.sum(c, axis=0, keepdims=True)

    thr = jnp.maximum(_topk_threshold(count_ge, (1, tq), kk), INT_MIN + 1)

    m_ref[...] = jnp.full(m_ref.shape, NEG, F32)
    l_ref[...] = jnp.zeros(l_ref.shape, F32)
    acc_ref[...] = jnp.zeros(acc_ref.shape, F32)

    def kv_step(kt, bias_of, thr_k):
        mask = keys_ref[kt] >= thr_k
        for h in range(N_HEADS):
            s = _dot(kh_ref[0, h, kt], qT_ref[0, 0, h]) + bias_of(h)
            m_old = m_ref[h]
            m_new = jnp.maximum(m_old, jnp.max(jnp.where(mask, s, NEG), axis=0, keepdims=True))
            p = jnp.where(mask, jnp.exp(s - m_new), 0.0)
            alpha = jnp.exp(m_old - m_new)
            l_ref[h] = alpha * l_ref[h] + jnp.sum(p, axis=0, keepdims=True)
            acc_ref[h] = alpha * acc_ref[h] + _dot(vT_ref[0, kt, h], p.astype(BF))
            m_ref[h] = m_new

    def far_body(kt, carry):
        kv_step(kt, lambda h: cb_ref[h], thr)
        return carry

    lax.fori_loop(0, jnp.maximum(nk - n_toep, 0), far_body, 0)
    for o in reversed(range(n_toep)):
        thr_o = jnp.where(qt >= o, thr, INT_MAX)
        kv_step(jnp.maximum(qt - o, 0), lambda h, o=o: toep_ref[o, h], thr_o)
    for h in range(N_HEADS):
        o_ref[0, h] = acc_ref[h] / l_ref[h]


def _attn_prompt(cb, qT, qiT, wiT, kh, vT, ki, toep, tq, kk):
    b, nq = qT.shape[:2]
    s = nq * tq
    n_toep = toep.shape[0]
    q_tile = lambda n: pl.BlockSpec((1, 1, N_HEADS, n, tq), lambda i, j: (i, j, 0, 0, 0))
    return pl.pallas_call(
        functools.partial(_attn_prompt_kernel, tq, n_toep, kk),
        grid=(b, nq),
        in_specs=[pl.BlockSpec(memory_space=pltpu.SMEM),
                  q_tile(HEAD_DIM), q_tile(IDX_DIM),
                  pl.BlockSpec((1, IDX_HEADS, tq), lambda i, j: (i, 0, j)),
                  pl.BlockSpec((1, N_HEADS, nq, tq, HEAD_DIM), lambda i, j: (i, 0, 0, 0, 0)),
                  pl.BlockSpec((1, nq, N_HEADS, HEAD_DIM, tq), lambda i, j: (i, 0, 0, 0, 0)),
                  pl.BlockSpec((1, nq, tq, IDX_DIM), lambda i, j: (i, 0, 0, 0)),
                  _full(toep.shape)],
        out_specs=pl.BlockSpec((1, N_HEADS, HEAD_DIM, tq), lambda i, j: (i, 0, 0, j)),
        out_shape=jax.ShapeDtypeStruct((b, N_HEADS, HEAD_DIM, s), F32),
        scratch_shapes=[pltpu.VMEM((nq, tq, tq), I32),
                        pltpu.VMEM((N_HEADS, 1, tq), F32), pltpu.VMEM((N_HEADS, 1, tq), F32),
                        pltpu.VMEM((N_HEADS, HEAD_DIM, tq), F32)],
        compiler_params=_params(2),
        name="attn_prompt",
    )(cb, qT, qiT, wiT, kh, vT, ki, toep)


PAGES_PER_STEP = 16


def _page_specs(block, n_pages, per_step=PAGES_PER_STEP):
    def spec(r):
        return pl.BlockSpec(block, lambda b, g, pt: (0, pt[b * n_pages + g * per_step + r]) + (0,) * (len(block) - 2))
    return [spec(r) for r in range(per_step)]


def _samp_scores_kernel(pt_ref, qi_ref, wi_ref, *refs):
    pages, o_ref = refs[:-1], refs[-1]
    qi = qi_ref[0]
    w = wi_ref[0] * IDX_HEADS ** -0.5
    rows = []
    for page in pages:
        s = _dot(qi, page[0, 0].astype(BF))
        rows.append(jnp.sum(jnp.maximum(s, 0.0) * w, axis=0, keepdims=True))
    o_ref[0] = jnp.concatenate(rows, axis=0)


def _samp_scores(page_table_flat, qi, wi, cache_idx_t, n_pages):
    db = qi.shape[0]
    grid_spec = pltpu.PrefetchScalarGridSpec(
        num_scalar_prefetch=1,
        grid=(db, 1),
        in_specs=[pl.BlockSpec((1, IDX_HEADS, IDX_DIM), lambda b, g, pt: (b, 0, 0)),
                  pl.BlockSpec((1, IDX_HEADS, 1), lambda b, g, pt: (b, 0, 0))]
                 + _page_specs((1, 1, IDX_DIM, PAGE_SIZE), n_pages, n_pages),
        out_specs=pl.BlockSpec((1, n_pages, PAGE_SIZE), lambda b, g, pt: (b, 0, 0)),
    )
    return pl.pallas_call(
        _samp_scores_kernel,
        grid_spec=grid_spec,
        out_shape=jax.ShapeDtypeStruct((db, n_pages, PAGE_SIZE), F32),
        compiler_params=_params(2),
        name="samp_scores",
    )(page_table_flat, qi, wi, *([cache_idx_t] * n_pages))


SEQ_PER_STEP = 8


def _total(x):
    return jnp.sum(jnp.sum(x, axis=0, keepdims=True), axis=1, keepdims=True)


def _ind(m):
    return jnp.where(m, 1.0, 0.0)


def _samp_select_kernel(kk, sc_ref, qi_ref, ki8_ref, kw_ref, tri_ref, lt_ref, sel_ref, nsel_ref,
                        keys_ref, thr_ref, knew_ref):
    key_new = []
    for r in range(SEQ_PER_STEP):
        prod = qi_ref[r].astype(F32) * ki8_ref[r].astype(F32)
        kwv = kw_ref[r]
        s_new = jnp.zeros((1, 1), F32)
        for h in range(IDX_HEADS):
            s = jnp.sum(prod[:, h * IDX_DIM:(h + 1) * IDX_DIM], axis=1, keepdims=True)
            s_new = s_new + jnp.maximum(s, 0.0) * (kwv[:, IDX_DIM + h:IDX_DIM + h + 1] * IDX_HEADS ** -0.5)
        key_new.append(_monotone_key(s_new))
        keys_ref[r] = _monotone_key(sc_ref[r])

    def bit_body(bi, thr_us):
        bit = jnp.left_shift(jnp.int32(1), 31 - bi)
        out = []
        for r in range(SEQ_PER_STEP):
            cand_u = thr_us[r] | bit
            cand = cand_u ^ INT_MIN
            cnt = _total(jnp.where(keys_ref[r] >= cand, 1, 0)) + jnp.where(key_new[r] >= cand, 1, 0)
            out.append(jnp.where(cnt >= kk, cand_u, thr_us[r]))
        return tuple(out)

    thr_us = lax.fori_loop(0, 32, bit_body, tuple(jnp.zeros((1, 1), I32) for _ in range(SEQ_PER_STEP)))
    for r in range(SEQ_PER_STEP):
        thr_ref[r] = jnp.broadcast_to(thr_us[r] ^ INT_MIN, (1, LANES))
        knew_ref[r] = jnp.broadcast_to(key_new[r], (1, LANES))

    def prefix(x):
        incl = _dot(x.astype(BF), tri_ref[...])
        page_tot = jnp.broadcast_to(incl[:, PAGE_SIZE - 1:PAGE_SIZE], incl.shape)
        return incl - x + _dot(lt_ref[...], page_tot.astype(BF))

    def seq_body(r, carry):
        keys = keys_ref[r]
        thr = thr_ref[r][:, :1]
        kn = knew_ref[r][:, :1]
        gt = keys > thr
        eq = keys == thr
        eq_f = _ind(eq)
        need = kk - (_total(_ind(gt)) + _ind(kn > thr))
        sel = jnp.where(gt, 1.0, jnp.where(eq, _ind(prefix(eq_f) < need), 0.0))
        new_sel = jnp.where(kn > thr, 1.0, jnp.where(kn == thr, _ind(_total(eq_f) < need), 0.0))
        sel_ref[r] = sel
        nsel_ref[r] = jnp.broadcast_to(new_sel, (1, LANES))
        return carry

    lax.fori_loop(0, SEQ_PER_STEP, seq_body, 0)


def _samp_select(scores, qib, ki8, kw, kk):
    db, n_pages, _ = scores.shape
    spb = SEQ_PER_STEP
    assert db % spb == 0
    pos = np.arange(PAGE_SIZE)
    tri = jnp.asarray(pos[:, None] <= pos[None, :], BF)
    pg = np.arange(n_pages)
    lt = jnp.asarray(pg[None, :] < pg[:, None], BF)
    per_seq = lambda n: pl.BlockSpec((spb, 1, n), lambda i: (i, 0, 0))
    return pl.pallas_call(
        functools.partial(_samp_select_kernel, kk),
        grid=(db // spb,),
        in_specs=[pl.BlockSpec((spb, n_pages, PAGE_SIZE), lambda i: (i, 0, 0)),
                  per_seq(qib.shape[2]), per_seq(ki8.shape[2]), per_seq(kw.shape[2]),
                  _full(tri.shape), _full(lt.shape)],
        out_specs=[pl.BlockSpec((spb, n_pages, PAGE_SIZE), lambda i: (i, 0, 0)), per_seq(LANES)],
        out_shape=[jax.ShapeDtypeStruct((db, n_pages, PAGE_SIZE), F32),
                   jax.ShapeDtypeStruct((db, 1, LANES), F32)],
        scratch_shapes=[pltpu.VMEM((spb, n_pages, PAGE_SIZE), I32),
                        pltpu.VMEM((spb, 1, LANES), I32), pltpu.VMEM((spb, 1, LANES), I32)],
        compiler_params=_params(1),
        name="samp_select",
    )(scores, qib, ki8, kw, tri, lt)


def _samp_attn_kernel(pt_ref, q_ref, qt_ref, sel_ref, bias_ref, bnew_ref, kn_ref, vnt_ref, *refs):
    k_pages = refs[:PAGES_PER_STEP]
    v_pages = refs[PAGES_PER_STEP:2 * PAGES_PER_STEP]
    o_ref, m_ref, l_ref, acc_ref, qb_ref = refs[2 * PAGES_PER_STEP:]
    g = pl.program_id(1)

    @pl.when(g == 0)
    def _():
        m_ref[...] = jnp.full(m_ref.shape, NEG, F32)
        l_ref[...] = jnp.zeros(l_ref.shape, F32)
        acc_ref[...] = jnp.zeros(acc_ref.shape, F32)
        for h in range(N_HEADS):
            qb_ref[h] = jnp.broadcast_to(qt_ref[0][:, h:h + 1], (HEAD_DIM, PAGE_SIZE))

    logits, masks = [], []
    for r in range(PAGES_PER_STEP):
        rows = [jnp.sum(k_pages[r][0, 0, h] * qb_ref[h], axis=0, keepdims=True) for h in range(N_HEADS)]
        mask = sel_ref[0, r:r + 1, :] > 0.5
        masks.append(mask)
        logits.append(jnp.where(mask, jnp.concatenate(rows, axis=0) + bias_ref[0, r], NEG))
    m_old = m_ref[...]
    m_new = m_old
    for s in logits:
        m_new = jnp.maximum(m_new, jnp.max(s, axis=1, keepdims=True))
    alpha = jnp.exp(m_old - m_new)
    probs = [jnp.where(mask, jnp.exp(s - m_new), 0.0) for s, mask in zip(logits, masks)]
    l = alpha * l_ref[...]
    for p in probs:
        l = l + jnp.sum(p, axis=1, keepdims=True)
    l_ref[...] = l
    m_ref[...] = m_new
    for h in range(N_HEADS):
        acc = acc_ref[h] * alpha[h:h + 1, :]
        for r in range(PAGES_PER_STEP):
            acc = acc + v_pages[r][0, 0, h] * probs[r][h:h + 1, :]
        acc_ref[h] = acc

    @pl.when(g == pl.num_programs(1) - 1)
    def _():
        s_new = jnp.sum(kn_ref[0] * q_ref[0], axis=1, keepdims=True) + bnew_ref[0]
        m_fin = jnp.maximum(m_new, s_new)
        a_fin = jnp.exp(m_new - m_fin)
        p_new = jnp.exp(s_new - m_fin)
        l_fin = a_fin * l + p_new
        cols = []
        for h in range(N_HEADS):
            past = jnp.sum(acc_ref[h], axis=1, keepdims=True) * a_fin[h:h + 1, :]
            cols.append((past + vnt_ref[0][:, h:h + 1] * p_new[h:h + 1, :]) / l_fin[h:h + 1, :])
        o_ref[0] = jnp.concatenate(cols, axis=1)


def _samp_attn(page_table_flat, q, qt, sel, bias_t, bias_new, kn, vnt, cache_k_t, cache_v_t, n_pages):
    db = q.shape[0]
    ng = n_pages // PAGES_PER_STEP
    per_seq = lambda shape: pl.BlockSpec((1,) + shape, lambda b, g, pt: (b,) + (0,) * len(shape))
    page_block = (1, 1, N_HEADS, HEAD_DIM, PAGE_SIZE)
    grid_spec = pltpu.PrefetchScalarGridSpec(
        num_scalar_prefetch=1,
        grid=(db, ng),
        in_specs=[per_seq((N_HEADS, HEAD_DIM)), per_seq((HEAD_DIM, N_HEADS)),
                  pl.BlockSpec((1, PAGES_PER_STEP, PAGE_SIZE), lambda b, g, pt: (b, g, 0)),
                  pl.BlockSpec((1, PAGES_PER_STEP, N_HEADS, PAGE_SIZE), lambda b, g, pt: (g, 0, 0, 0)),
                  per_seq((N_HEADS, 1)), per_seq((N_HEADS, HEAD_DIM)), per_seq((HEAD_DIM, N_HEADS))]
                 + _page_specs(page_block, n_pages) + _page_specs(page_block, n_pages),
        out_specs=per_seq((HEAD_DIM, N_HEADS)),
        scratch_shapes=[pltpu.VMEM((N_HEADS, 1), F32), pltpu.VMEM((N_HEADS, 1), F32),
                        pltpu.VMEM((N_HEADS, HEAD_DIM, PAGE_SIZE), F32),
                        pltpu.VMEM((N_HEADS, HEAD_DIM, PAGE_SIZE), F32)],
    )
    return pl.pallas_call(
        _samp_attn_kernel,
        grid_spec=grid_spec,
        out_shape=jax.ShapeDtypeStruct((db, HEAD_DIM, N_HEADS), F32),
        compiler_params=_params(2),
        name="samp_attn",
    )(page_table_flat, q, qt, sel, bias_t, bias_new, kn, vnt,
      *([cache_k_t] * PAGES_PER_STEP), *([cache_v_t] * PAGES_PER_STEP))


def _merge_kernel(per_row, x_ref, yc_ref, at_ref, sgc_ref, sga_ref, mod_ref, lng_ref, lnb_ref,
                  wco_ref, wao_ref, wout_ref, g2_ref, wpq_ref, kbd_ref,
                  xm_ref, h2t_ref, st_ref):
    yc = yc_ref[...]
    mu = jnp.mean(yc, axis=-1, keepdims=True)
    yd = yc - mu
    ln = yd * lax.rsqrt(jnp.mean(yd * yd, axis=-1, keepdims=True) + EPS) * lng_ref[...] + lnb_ref[...]
    act = ln * jax.nn.sigmoid(ln)
    conv_out = _dot(act.astype(BF), wco_ref[...])
    attn_out = _dot(at_ref[0].T.astype(BF), wao_ref[...])
    merged = sgc_ref[...] * conv_out + sga_ref[...] * attn_out
    xm = x_ref[...] + _mod(mod_ref, 2, per_row) * _dot(merged.astype(BF), wout_ref[...])
    xm_ref[...] = xm
    rs = lax.rsqrt(jnp.mean(xm * xm, axis=-1, keepdims=True) + EPS)
    h2 = (xm * rs) * g2_ref[...] * (1.0 + _mod(mod_ref, 4, per_row)) + _mod(mod_ref, 3, per_row)
    h2t_ref[...] = pltpu.bitcast(h2.T.astype(BF), I32)
    pq = _dot(h2.astype(BF), wpq_ref[...])
    st_ref[...] = _dot_nt(kbd_ref[...], pq.astype(BF))


def _merge(x, yc, at, sgc, sga, mod, per_row, rows_per_seq, tm, wts):
    t = x.shape[0]
    row = lambda n: pl.BlockSpec((tm, n), lambda i: (i, 0))
    col = lambda n: pl.BlockSpec((n, tm), lambda i: (0, i))
    n_scores = PEER_HEADS * 2 * PEER_KEYS
    return pl.pallas_call(
        functools.partial(_merge_kernel, per_row),
        grid=(t // tm,),
        in_specs=[row(D_MODEL), row(CONV_DIM),
                  pl.BlockSpec((1, ATTN_DIM, tm), lambda i: (i // (at.shape[2] // tm), 0, i % (at.shape[2] // tm))),
                  row(D_MODEL), row(D_MODEL),
                  _mod_spec(per_row, tm, rows_per_seq)] + [_full(w.shape) for w in wts],
        out_specs=[row(D_MODEL), col(D_MODEL // 2), col(n_scores)],
        out_shape=[jax.ShapeDtypeStruct((t, D_MODEL), F32),
                   jax.ShapeDtypeStruct((D_MODEL // 2, t), I32),
                   jax.ShapeDtypeStruct((n_scores, t), F32)],
        compiler_params=_params(1),
        name="merge",
    )(x, yc, at, sgc, sga, mod, *wts)


N_EXTRACT = PEER_TOPK + 1


def _top_extract(cur, n):
    outs = []
    for _ in range(n):
        mx = jnp.max(cur, axis=0, keepdims=True)
        outs.append(mx)
        cur = jnp.where(cur >= mx, -jnp.inf, cur)
    return outs


RANK_NONE = float(PEER_KEYS - 1)


def _peer_select_kernel(lb, st_ref, r1_ref, e1_ref, n_ref, al_ref):
    tl = st_ref.shape[1]

    def head_body(h, carry):
        base = pl.multiple_of(h * 2 * PEER_KEYS, 2 * PEER_KEYS)
        s0 = st_ref[pl.ds(base, PEER_KEYS), :]
        s1 = st_ref[pl.ds(base + PEER_KEYS, PEER_KEYS), :]
        a0 = _top_extract(s0, N_EXTRACT)
        a1 = []
        cur = s1
        rank1 = jnp.full(s1.shape, RANK_NONE, F32)
        for k in range(N_EXTRACT):
            mx = jnp.max(cur, axis=0, keepdims=True)
            a1.append(mx)
            hit = cur >= mx
            if k < PEER_TOPK:
                rank1 = jnp.where(hit, float(k), rank1)
            cur = jnp.where(hit, -jnp.inf, cur)
        pad = [jnp.full((1, tl), -jnp.inf, F32)] * (3 * SUBLANES - N_EXTRACT)
        a1_all = jnp.concatenate(a1 + pad, axis=0)
        cands = [a0[0] + a1_all] + [a0[k] + a1_all[:SUBLANES] for k in range(1, N_EXTRACT)]
        top = _top_extract(jnp.concatenate(cands, axis=0), N_EXTRACT)
        thr = 0.5 * (top[PEER_TOPK - 1] + top[PEER_TOPK])
        z = jnp.zeros_like(thr)
        for k in range(PEER_TOPK):
            z = z + jnp.exp(top[k] - top[0])
        tau = thr - s0
        cnt = jnp.zeros(s0.shape, F32)
        for l in range(PEER_TOPK):
            cnt = cnt + jnp.where(a1[l] >= tau, 1.0, 0.0)
        e1 = jnp.exp(s1 - a1[0])
        al = jnp.exp(s0 - a0[0]) / z
        for tb in range(tl // lb):
            ts = slice(tb * lb, (tb + 1) * lb)
            hi_bits = lambda v: lax.bitcast_convert_type(v[:, ts].astype(BF).astype(F32), I32)
            both = lambda b: b | lax.shift_right_logical(b, 16)
            half = PEER_KEYS // 2
            pair = lambda b: lax.shift_right_logical(b[:half], 16) | b[half:]
            r1_ref[h, tb] = both(hi_bits(rank1))
            e1_ref[h, tb] = both(hi_bits(e1))
            n_ref[h, tb] = pair(hi_bits(cnt))
            al_ref[h, tb] = pair(hi_bits(al))
        return carry

    lax.fori_loop(0, PEER_HEADS, head_body, 0)


def _peer_select(st, tl):
    t = st.shape[1]
    lb = min(LANES, t)
    spec = lambda rows: pl.BlockSpec((PEER_HEADS, tl // lb, rows, lb), lambda i: (0, i, 0, 0))
    shp = lambda rows: jax.ShapeDtypeStruct((PEER_HEADS, t // lb, rows, lb), I32)
    per_j, per_i = PEER_KEYS, PEER_KEYS // 2
    return pl.pallas_call(
        functools.partial(_peer_select_kernel, lb),
        grid=(t // tl,),
        in_specs=[pl.BlockSpec((st.shape[0], tl), lambda i: (0, i))],
        out_specs=[spec(per_j), spec(per_j), spec(per_i), spec(per_i)],
        out_shape=[shp(per_j), shp(per_j), shp(per_i), shp(per_i)],
        compiler_params=_params(1),
        name="peer_select",
    )(st)


PEER_PAIRS_PER_STEP = SUBLANES


def _peer_main_kernel(per_row, h2t_ref, *refs):
    n_units = PEER_PAIRS_PER_STEP // 2
    u_refs = (refs[:n_units], refs[n_units:2 * n_units])
    vt_refs = (refs[2 * n_units:3 * n_units], refs[3 * n_units:4 * n_units])
    r1_ref, e1_ref, n_ref, al_ref, xm_ref, mod_ref, o_ref, acc_ref, a_ref, g_ref = refs[4 * n_units:]
    j = pl.program_id(1)
    tm = h2t_ref.shape[1]
    n_tb, _, lb = r1_ref.shape[1:]

    @pl.when(j == 0)
    def _():
        acc_ref[...] = jnp.zeros(acc_ref.shape, F32)

    i0 = pl.multiple_of(j * PEER_PAIRS_PER_STEP, PEER_PAIRS_PER_STEP)
    h2t = pltpu.bitcast(h2t_ref[...], BF)
    unit = 2 * PEER_KEYS
    jn = PEER_KEYS // 2

    def activations(un):
        for c in range(2):
            a_ref[un % 2, c] = _dot(pltpu.bitcast(u_refs[c][un][...], BF), h2t)

    activations(0)
    for un in range(n_units):
        rows = slice(un * unit, (un + 1) * unit)
        if un + 1 < n_units:
            activations(un + 1)
        for tb in range(n_tb):
            ts = slice(tb * lb, (tb + 1) * lb)
            for jb in range(PEER_KEYS // jn):
                js = slice(jb * jn, (jb + 1) * jn)
                w = [jnp.zeros((2 * jn, lb), BF), jnp.zeros((2 * jn, lb), BF)]
                for h in range(PEER_HEADS):
                    cnt = n_ref[h, tb, pl.ds(i0, PEER_PAIRS_PER_STEP), :]
                    al = al_ref[h, tb, pl.ds(i0, PEER_PAIRS_PER_STEP), :]
                    r1 = pltpu.bitcast(r1_ref[h, tb, js, :], BF)
                    e1 = pltpu.bitcast(e1_ref[h, tb, js, :], BF)
                    for q in range(2):
                        pp = 2 * un + q
                        cnt_b = pltpu.bitcast(jnp.broadcast_to(cnt[pp:pp + 1], (jn, lb)), BF)
                        al_b = pltpu.bitcast(jnp.broadcast_to(al[pp:pp + 1], (jn, lb)), BF)
                        w[q] = w[q] + jnp.where(r1 < cnt_b, e1 * al_b, 0)
                for q in range(2):
                    wi = pltpu.bitcast(w[q], I32)
                    halves = (lax.bitcast_convert_type(lax.shift_left(wi, 16), F32),
                              lax.bitcast_convert_type(wi & jnp.int32(-65536), F32))
                    for c in range(2):
                        lo = q * PEER_KEYS + jb * jn
                        a = a_ref[un % 2, c, lo:lo + jn, ts]
                        gelu = 0.5 * a * (1.0 + lax.erf(a * (2.0 ** -0.5)))
                        g_ref[c, un * unit + lo:un * unit + lo + jn, ts] = (halves[c] * gelu).astype(BF)
        acc_ref[...] += (_dot(pltpu.bitcast(vt_refs[0][un][...], BF), g_ref[0, rows, :])
                         + _dot(pltpu.bitcast(vt_refs[1][un][...], BF), g_ref[1, rows, :]))

    @pl.when(j == pl.num_programs(1) - 1)
    def _():
        o_ref[...] = xm_ref[...] + _mod(mod_ref, 5, per_row) * acc_ref[...].T


def _peer_main(h2t, u_pk, vt_pk, r1, e1, cnt, al, xm, mod, per_row, rows_per_seq, tm):
    t = xm.shape[0]
    ec = PEER_PAIRS_PER_STEP * PEER_KEYS
    n_steps = u_pk.shape[0] * 2 // (2 * ec)
    n_units = PEER_PAIRS_PER_STEP // 2
    unit = 2 * PEER_KEYS
    units_lo = range(n_units)
    units_hi = range(n_steps * n_units, (n_steps + 1) * n_units)
    lb = r1.shape[3]
    sel = lambda rows: pl.BlockSpec((PEER_HEADS, tm // lb, rows, lb), lambda i, j: (0, i, 0, 0))
    if per_row:
        mod_spec = pl.BlockSpec((6, tm, D_MODEL), lambda i, j: (0, i, 0))
    else:
        tiles_per_seq = rows_per_seq // tm
        mod_spec = pl.BlockSpec((1, 6, D_MODEL), lambda i, j: (i // tiles_per_seq, 0, 0))
    return pl.pallas_call(
        functools.partial(_peer_main_kernel, per_row),
        grid=(t // tm, n_steps),
        in_specs=[pl.BlockSpec((D_MODEL // 2, tm), lambda i, j: (0, i))]
                 + [pl.BlockSpec((unit // 2, D_MODEL), lambda i, j, k=k: (j * n_units + k, 0)) for k in units_lo]
                 + [pl.BlockSpec((unit // 2, D_MODEL), lambda i, j, k=k: (j * n_units + k, 0)) for k in units_hi]
                 + [pl.BlockSpec((D_MODEL // 2, unit), lambda i, j, k=k: (0, j * n_units + k)) for k in units_lo]
                 + [pl.BlockSpec((D_MODEL // 2, unit), lambda i, j, k=k: (0, j * n_units + k)) for k in units_hi]
                 + [sel(PEER_KEYS), sel(PEER_KEYS), sel(PEER_KEYS // 2), sel(PEER_KEYS // 2),
                  pl.BlockSpec((tm, D_MODEL), lambda i, j: (i, 0)),
                  mod_spec],
        out_specs=pl.BlockSpec((tm, D_MODEL), lambda i, j: (i, 0)),
        out_shape=jax.ShapeDtypeStruct((t, D_MODEL), F32),
        scratch_shapes=[pltpu.VMEM((D_MODEL, tm), F32), pltpu.VMEM((2, 2, 2 * PEER_KEYS, tm), F32),
                        pltpu.VMEM((2, ec, tm), BF)],
        compiler_params=_params(2),
        name="peer_main",
    )(h2t, *([u_pk] * (2 * n_units)), *([vt_pk] * (2 * n_units)), r1, e1, cnt, al, xm, mod)


PACK_ROWS = 512


def _peer_pack_kernel(u_ref, v_ref, u_pk_ref, vt_pk_ref):
    u_pk_ref[...] = pltpu.bitcast(u_ref[...].astype(BF), I32)
    vt_pk_ref[...] = pltpu.bitcast(v_ref[...].T.astype(BF), I32)


def _peer_pack(u, v):
    n_e = u.shape[0]
    return pl.pallas_call(
        _peer_pack_kernel,
        grid=(n_e // PACK_ROWS,),
        in_specs=[pl.BlockSpec((PACK_ROWS, D_MODEL), lambda i: (i, 0))] * 2,
        out_specs=[pl.BlockSpec((PACK_ROWS // 2, D_MODEL), lambda i: (i, 0)),
                   pl.BlockSpec((D_MODEL // 2, PACK_ROWS), lambda i: (0, i))],
        out_shape=[jax.ShapeDtypeStruct((n_e // 2, D_MODEL), I32),
                   jax.ShapeDtypeStruct((D_MODEL // 2, n_e), I32)],
        compiler_params=_params(1),
        name="peer_pack",
    )(u, v)


def _tile(n, pref):
    return pref if n % pref == 0 else n


def kernel(x_prompt, x_sample, cache_k, cache_v, cache_idx_k, state_conv, page_table, c_prompt, c_sample,
           w_ada, b_ada, norm1_g, w_in, q_norm_g, k_norm_g, rel_bias, w_dw, b_dw, conv_ln_g, conv_ln_b,
           w_conv_o, w_attn_o, w_out, norm2_g, w_peer_q, peer_keys, peer_u, peer_v):
    b, s, _ = x_prompt.shape
    db, ds, _ = x_sample.shape
    n_pages = page_table.shape[1]
    past = n_pages * PAGE_SIZE
    assert w_ada.shape[0] == 1, "one layer"
    assert ds == 1, "one new token per sample sequence"
    assert n_pages % PAGES_PER_STEP == 0
    tp = b * s

    w = w_in[0]
    pts = np.cumsum([0, 2 * CONV_DIM, ATTN_DIM, ATTN_DIM, ATTN_DIM, IDX_HEADS * IDX_DIM, IDX_DIM, IDX_HEADS,
                     D_MODEL, D_MODEL])
    seg = lambda i: w[:, pts[i]:pts[i + 1]].astype(BF)
    w_kw = jnp.pad(w[:, pts[5]:pts[7]], ((0, 0), (0, LANES - IDX_DIM - IDX_HEADS))).astype(BF)
    head_of = np.arange(ATTN_DIM) // HEAD_DIM
    bd = jnp.asarray(head_of[:, None] == head_of[None, :], BF)
    row = lambda v: v.reshape(1, -1)
    in_wts = [row(norm1_g[0]), seg(0), seg(1), seg(2), seg(3), seg(4), w_kw, seg(7), seg(8),
              row(jnp.tile(q_norm_g[0], N_HEADS)), row(jnp.tile(k_norm_g[0], N_HEADS)), bd]
    n_half = PEER_HEADS * 2
    kb16 = peer_keys[0].reshape(n_half, PEER_KEYS, PEER_HALF)
    kbd_t = (jnp.eye(n_half, dtype=F32)[:, None, :, None] * kb16[:, :, None, :]).reshape(
        n_half * PEER_KEYS, n_half * PEER_HALF).astype(BF)
    merge_wts = [row(conv_ln_g[0]), row(conv_ln_b[0]), w_conv_o[0].astype(BF), w_attn_o[0].astype(BF),
                 w_out[0].astype(BF), row(norm2_g[0]), w_peer_q[0].astype(BF), kbd_t]
    u_pk, vt_pk = _peer_pack(peer_u[0], peer_v[0])

    mod = _ada(jnp.concatenate([c_prompt, c_sample], axis=0), w_ada[0].astype(BF), row(b_ada[0]))
    mod_p = mod[:b].reshape(b, 6, D_MODEL)
    mod_s = mod[b:].reshape(db, 6, D_MODEL).transpose(1, 0, 2)

    tq = _tile(s, 256)
    nq = s // tq
    glu_p, sgc_p, sga_p, kT_p, vT32_p, kiT_p, qT_p, qiT_p, vT_p, kh_p, ki_p, wiT_p = _inproj(
        x_prompt.reshape(tp, D_MODEL), mod_p, False, s, tq, in_wts)
    glu_s, sgc_s, sga_s, k_s, v_s, kw_s, qb_s, qib_s = _inproj(
        x_sample.reshape(db, D_MODEL), mod_s, True, 1, db, in_wts)

    yc_p = _conv_prompt(glu_p.reshape(b, s, CONV_DIM), w_dw[0], row(b_dw[0]), _tile(s, 256)).reshape(tp, CONV_DIM)
    yc_s = _conv_sample(state_conv[0].transpose(1, 0, 2), glu_s, w_dw[0], row(b_dw[0]))

    bucket =_rel_bucket_np(np.arange(max(s, past + 1)))
    n_toep = nq
    while n_toep > 0 and len(set(bucket[max((n_toep - 1) * tq - (tq - 1), 0):n_toep * tq])) == 1 \
            and bucket[(n_toep - 1) * tq] == bucket[s - 1]:
        n_toep -= 1
    n_toep = max(n_toep, 1)

    def bias_of_bucket(bkt):
        onehot = jnp.asarray(np.eye(REL_BUCKETS, dtype=np.float32)[bkt.reshape(-1)])
        return jnp.dot(onehot, rel_bias, precision=lax.Precision.HIGHEST).reshape(bkt.shape + (N_HEADS,))

    width = 2 * tq + 1
    dist = (np.arange(n_toep)[:, None] - 1) * tq + np.arange(width)[None, :]
    table = bias_of_bucket(_rel_bucket_np(dist)).transpose(0, 2, 1)
    skew = jnp.tile(table, (1, 1, tq))[:, :, :tq * (width - 1)].reshape(n_toep, N_HEADS, tq, width - 1)
    toep = skew[:, :, :, tq:]
    cb = rel_bias[int(bucket[s - 1])]

    kk_p = min(IDX_TOPK_MAX, s // 4)
    split_heads = lambda a: a.reshape(b, nq, N_HEADS, -1, tq)
    at_p = _attn_prompt(cb, split_heads(qT_p), split_heads(qiT_p), wiT_p, kh_p, split_heads(vT_p), ki_p,
                        toep, tq, kk_p)
    at_p = at_p.reshape(b, ATTN_DIM, s)

    kk_s = min(IDX_TOPK_MAX, (past + ds) // 4)
    pt_flat = page_table.reshape(-1)
    sc = _samp_scores(pt_flat, qib_s.reshape(db, IDX_HEADS, IDX_DIM),
                      kw_s[:, IDX_DIM:IDX_DIM + IDX_HEADS].reshape(db, IDX_HEADS, 1),
                      cache_idx_k.transpose(0, 1, 3, 2), n_pages)
    ki8 = jnp.tile(kw_s[:, :IDX_DIM].astype(BF), (1, IDX_HEADS))
    sel_s, new_sel = _samp_select(sc.reshape(db, n_pages, PAGE_SIZE), qib_s[:, None, :], ki8[:, None, :],
                                  kw_s[:, None, :], kk_s)
    pos = np.arange(past).reshape(n_pages // PAGES_PER_STEP, PAGES_PER_STEP, PAGE_SIZE)
    bias_t = bias_of_bucket(_rel_bucket_np(past - pos)).transpose(0, 1, 3, 2)
    bias_new = jnp.where(new_sel[:, 0, :1] > 0.0, rel_bias[int(_rel_bucket_np(np.zeros((), np.int64)))][None], NEG)
    heads = lambda a: a.reshape(db, N_HEADS, HEAD_DIM)
    q_s = heads(qb_s.astype(F32))
    at_s = _samp_attn(pt_flat, q_s, q_s.transpose(0, 2, 1), sel_s, bias_t, bias_new.reshape(db, N_HEADS, 1),
                      heads(k_s), heads(v_s).transpose(0, 2, 1),
                      cache_k.transpose(0, 1, 3, 4, 2), cache_v.transpose(0, 1, 3, 4, 2), n_pages)
    at_s = at_s.transpose(2, 1, 0).reshape(1, ATTN_DIM, db)

    def tail(x, yc, at, sgc, sga, mod_g, per_row, rows_per_seq, tm_merge, tl_sel, tm_peer):
        xm, h2t, st = _merge(x, yc, at, sgc, sga, mod_g, per_row, rows_per_seq, tm_merge, merge_wts)
        s1, e1, tau, al = _peer_select(st, tl_sel)
        return _peer_main(h2t, u_pk, vt_pk, s1, e1, tau, al, xm, mod_g, per_row, rows_per_seq, tm_peer)

    y_p = tail(x_prompt.reshape(tp, D_MODEL), yc_p, at_p, sgc_p, sga_p, mod_p, False, s,
               _tile(s, 256), _tile(tp, 256), _tile(s, 512))
    y_s = tail(x_sample.reshape(db, D_MODEL), yc_s, at_s, sgc_s, sga_s, mod_s, True, 1, db, db, db)

    glu_p3 = glu_p.reshape(b, s, CONV_DIM)
    conv_s = jnp.concatenate([state_conv[0], glu_s[:, None, :]], axis=1)[:, -CONV_STATE:]
    seq_major = lambda a: a.reshape(b, N_HEADS, HEAD_DIM, s).transpose(0, 3, 1, 2)[None]
    return (y_p.reshape(b, s, D_MODEL), y_s.reshape(db, ds, D_MODEL),
            seq_major(kT_p), seq_major(vT32_p),
            kiT_p.transpose(0, 2, 1)[None], glu_p3[None, :, -CONV_STATE:],
            k_s.reshape(1, db, ds, N_HEADS, HEAD_DIM), v_s.reshape(1, db, ds, N_HEADS, HEAD_DIM),
            kw_s[:, :IDX_DIM].reshape(1, db, ds, IDX_DIM), conv_s[None])
```

```python
import functools
import math

import numpy as np
import jax
import jax.numpy as jnp
from jax import lax
from jax.experimental import pallas as pl
from jax.experimental.pallas import tpu as pltpu

F32 = jnp.float32
BF = jnp.bfloat16
I32 = jnp.int32

D_MODEL = 1024
N_HEADS = 8
HEAD_DIM = 64
ATTN_DIM = N_HEADS * HEAD_DIM
IDX_HEADS = 8
IDX_DIM = 64
IDX_TOPK_MAX = 256
REL_BUCKETS = 32
REL_MAX_DIST = 128
CONV_DIM = 512
CONV_WIDTH = 31
CONV_STATE = CONV_WIDTH - 1
PEER_HEADS = 8
PEER_KEYS = 128
PEER_HALF = 64
PEER_TOPK = 16
PAGE_SIZE = 128
EPS = 1e-6

NEG = -1e30
INT_MIN = -(2 ** 31)
INT_MAX = 2 ** 31 - 1
LANES = 128
SUBLANES = 8
VMEM_LIMIT = 52 * 1024 * 1024


def _params(n_axes):
    return pltpu.CompilerParams(dimension_semantics=("arbitrary",) * n_axes,
                                vmem_limit_bytes=VMEM_LIMIT)


def _full(shape):
    zeros = (0,) * len(shape)
    return pl.BlockSpec(shape, lambda *_: zeros)


def _mod(mod_ref, k, per_row):
    return mod_ref[k] if per_row else mod_ref[0, k:k + 1, :]


def _mod_spec(per_row, tm, rows_per_seq):
    if per_row:
        return pl.BlockSpec((6, tm, D_MODEL), lambda i: (0, i, 0))
    tiles_per_seq = rows_per_seq // tm
    return pl.BlockSpec((1, 6, D_MODEL), lambda i: (i // tiles_per_seq, 0, 0))


def _dot(a, b):
    return jnp.dot(a, b, preferred_element_type=F32)


def _dot_nt(a, b):
    return lax.dot_general(a, b, (((1,), (1,)), ((), ())), preferred_element_type=F32)


def _monotone_key(x):
    bits = lax.bitcast_convert_type(x, I32)
    return bits ^ ((bits >> 31) & INT_MAX)


def _ada_kernel(c_ref, w_ref, b_ref, o_ref):
    c = c_ref[...]
    a = (c * jax.nn.sigmoid(c)).astype(BF)
    o_ref[...] = _dot(a, w_ref[...]) + b_ref[...]


def _ada(c, w_bf, b):
    rows = c.shape[0]
    n = w_bf.shape[1]
    nb = 1536
    return pl.pallas_call(
        _ada_kernel,
        grid=(n // nb,),
        in_specs=[_full((rows, D_MODEL)),
                  pl.BlockSpec((D_MODEL, nb), lambda j: (0, j)),
                  pl.BlockSpec((1, nb), lambda j: (0, j))],
        out_specs=pl.BlockSpec((rows, nb), lambda j: (0, j)),
        out_shape=jax.ShapeDtypeStruct((rows, n), F32),
        compiler_params=_params(1),
        name="ada",
    )(c, w_bf, b)


def _inproj_kernel(per_row, x_ref, mod_ref, g1_ref, wglu_ref, wq_ref, wk_ref, wv_ref, wqi_ref,
                   wkw_ref, wgc_ref, wga_ref, gq_ref, gk_ref, bd_ref,
                   glu_ref, sgc_ref, sga_ref, *attn_refs):
    x = x_ref[...]
    shift = _mod(mod_ref, 0, per_row)
    scale = _mod(mod_ref, 1, per_row)
    rs = lax.rsqrt(jnp.mean(x * x, axis=-1, keepdims=True) + EPS)
    h = (x * rs) * g1_ref[...] * (1.0 + scale) + shift
    hb = h.astype(BF)

    def head_norm(z, g_ref):
        sq = z * z
        hi = sq.astype(BF)
        lo = (sq - hi.astype(F32)).astype(BF)
        ss = _dot(hi, bd_ref[...]) + _dot(lo, bd_ref[...])
        return z * lax.rsqrt(ss * (1.0 / HEAD_DIM) + EPS) * g_ref[...]

    gl = _dot(hb, wglu_ref[...])
    glu_ref[...] = gl[:, :CONV_DIM] * jax.nn.sigmoid(gl[:, CONV_DIM:])
    q = head_norm(_dot(hb, wq_ref[...]), gq_ref)
    k = head_norm(_dot(hb, wk_ref[...]), gk_ref)
    v = _dot(hb, wv_ref[...])
    qs = q * HEAD_DIM ** -0.5
    qis = _dot(hb, wqi_ref[...]) * IDX_DIM ** -0.5
    kw = _dot(hb, wkw_ref[...])
    sgc_ref[...] = jax.nn.sigmoid(_dot(hb, wgc_ref[...]))
    sga_ref[...] = jax.nn.sigmoid(_dot(hb, wga_ref[...]))
    if per_row:
        k_ref, v_ref, kw_ref, qb_ref, qib_ref = attn_refs
        k_ref[...] = k
        v_ref[...] = v
        kw_ref[...] = kw
        qb_ref[...] = qs.astype(BF)
        qib_ref[...] = qis.astype(BF)
    else:
        kT_ref, vT_ref, kiT_ref, qTb_ref, qiTb_ref, vTb_ref, kh_ref, ki_ref, wiT_ref = attn_refs
        v_t = v.T
        kw_t = kw.T
        kT_ref[0] = k.T
        vT_ref[0] = v_t
        kiT_ref[0] = kw_t[:IDX_DIM, :]
        qTb_ref[0, 0] = qs.T.astype(BF)
        qiTb_ref[0, 0] = qis.T.astype(BF)
        vTb_ref[0, 0] = v_t.astype(BF)
        for hd in range(N_HEADS):
            kh_ref[0, hd, 0] = k[:, hd * HEAD_DIM:(hd + 1) * HEAD_DIM].astype(BF)
        ki_ref[0, 0] = kw[:, :IDX_DIM].astype(BF)
        wiT_ref[0] = kw_t[IDX_DIM:IDX_DIM + IDX_HEADS, :]


def _inproj(x, mod, per_row, rows_per_seq, tm, wts):
    t = x.shape[0]
    row = lambda n: pl.BlockSpec((tm, n), lambda i: (i, 0))
    rows = lambda n, dt: jax.ShapeDtypeStruct((t, n), dt)
    out_specs = [row(CONV_DIM), row(D_MODEL), row(D_MODEL)]
    out_shape = [rows(CONV_DIM, F32), rows(D_MODEL, F32), rows(D_MODEL, F32)]
    if per_row:
        out_specs += [row(ATTN_DIM), row(ATTN_DIM), row(LANES), row(ATTN_DIM), row(ATTN_DIM)]
        out_shape += [rows(ATTN_DIM, F32), rows(ATTN_DIM, F32), rows(LANES, F32), rows(ATTN_DIM, BF), rows(ATTN_DIM, BF)]
    else:
        nq = rows_per_seq // tm
        b = t // rows_per_seq
        seq_minor = lambda n: pl.BlockSpec((1, n, tm), lambda i: (i // nq, 0, i % nq))
        minor = pl.BlockSpec((1, 1, ATTN_DIM, tm), lambda i: (i // nq, i % nq, 0, 0))
        minor_shape = jax.ShapeDtypeStruct((b, nq, ATTN_DIM, tm), BF)
        out_specs += [seq_minor(ATTN_DIM), seq_minor(ATTN_DIM), seq_minor(IDX_DIM),
                      minor, minor, minor,
                      pl.BlockSpec((1, N_HEADS, 1, tm, HEAD_DIM), lambda i: (i // nq, 0, i % nq, 0, 0)),
                      pl.BlockSpec((1, 1, tm, IDX_DIM), lambda i: (i // nq, i % nq, 0, 0)),
                      seq_minor(IDX_HEADS)]
        out_shape += [jax.ShapeDtypeStruct((b, ATTN_DIM, rows_per_seq), F32),
                      jax.ShapeDtypeStruct((b, ATTN_DIM, rows_per_seq), F32),
                      jax.ShapeDtypeStruct((b, IDX_DIM, rows_per_seq), F32),
                      minor_shape, minor_shape, minor_shape,
                      jax.ShapeDtypeStruct((b, N_HEADS, nq, tm, HEAD_DIM), BF),
                      jax.ShapeDtypeStruct((b, nq, tm, IDX_DIM), BF),
                      jax.ShapeDtypeStruct((b, IDX_HEADS, rows_per_seq), F32)]
    return pl.pallas_call(
        functools.partial(_inproj_kernel, per_row),
        grid=(t // tm,),
        in_specs=[row(D_MODEL), _mod_spec(per_row, tm, rows_per_seq)] + [_full(w.shape) for w in wts],
        out_specs=out_specs,
        out_shape=out_shape,
        compiler_params=_params(1),
        name="inproj",
    )(x, mod, *wts)


CONV_HALO = 32


def _conv_prompt_kernel(tt, glu_ref, wdw_ref, bdw_ref, y_ref, ext_ref):
    j = pl.program_id(1)

    @pl.when(j == 0)
    def _():
        ext_ref[0:CONV_HALO, :] = jnp.zeros((CONV_HALO, CONV_DIM), F32)
        ext_ref[CONV_HALO:, :] = glu_ref[0]

    n = tt + CONV_HALO
    t0 = pl.multiple_of(j * tt, tt)
    win = ext_ref[pl.ds(t0, n), :]
    acc = jnp.zeros((tt, CONV_DIM), F32) + bdw_ref[...]
    lead = CONV_HALO - CONV_STATE
    for r in range(SUBLANES):
        rolled = win if r == 0 else pltpu.roll(win, n - r, axis=0)
        for tap in range(CONV_WIDTH):
            off = lead + tap
            if off % SUBLANES == r:
                a = off - r
                acc = acc + wdw_ref[tap:tap + 1, :] * rolled[a:a + tt]
    y_ref[0] = acc


def _conv_prompt(glu, wdw, bdw, tt):
    b, s, _ = glu.shape
    return pl.pallas_call(
        functools.partial(_conv_prompt_kernel, tt),
        grid=(b, s // tt),
        in_specs=[pl.BlockSpec((1, s, CONV_DIM), lambda i, j: (i, 0, 0)),
                  _full(wdw.shape), _full(bdw.shape)],
        out_specs=pl.BlockSpec((1, tt, CONV_DIM), lambda i, j: (i, j, 0)),
        out_shape=jax.ShapeDtypeStruct((b, s, CONV_DIM), F32),
        scratch_shapes=[pltpu.VMEM((s + CONV_HALO, CONV_DIM), F32)],
        compiler_params=_params(2),
        name="conv_prompt",
    )(glu, wdw, bdw)


def _conv_sample_kernel(st_ref, glu_ref, wdw_ref, bdw_ref, y_ref):
    acc = bdw_ref[...] + wdw_ref[CONV_STATE:CONV_STATE + 1, :] * glu_ref[...]
    for j in range(CONV_STATE):
        acc = acc + wdw_ref[j:j + 1, :] * st_ref[j]
    y_ref[...] = acc


def _conv_sample(state_t, glu, wdw, bdw):
    db = glu.shape[0]
    return pl.pallas_call(
        _conv_sample_kernel,
        grid=(1,),
        in_specs=[_full(state_t.shape), _full(glu.shape), _full(wdw.shape), _full(bdw.shape)],
        out_specs=_full((db, CONV_DIM)),
        out_shape=jax.ShapeDtypeStruct((db, CONV_DIM), F32),
        compiler_params=_params(1),
        name="conv_sample",
    )(state_t, glu, wdw, bdw)


def _rel_bucket_np(dist):
    max_exact = REL_BUCKETS // 2
    d = np.maximum(dist, 0)
    df = np.maximum(d, 1).astype(np.float32)
    large = max_exact + (np.log(df / max_exact) / math.log(REL_MAX_DIST / max_exact)
                         * (REL_BUCKETS - max_exact)).astype(np.int32)
    return np.where(d < max_exact, d, np.minimum(large, REL_BUCKETS - 1)).astype(np.int32)


def _topk_threshold(count_ge, shape, kk):
    def bit_body(bi, thr_u):
        cand_u = thr_u | jnp.left_shift(jnp.int32(1), 31 - bi)
        cnt = count_ge(cand_u ^ INT_MIN)
        return jnp.where(cnt >= kk, cand_u, thr_u)

    thr_u = lax.fori_loop(0, 32, bit_body, jnp.zeros(shape, I32))
    return thr_u ^ INT_MIN


def _attn_prompt_kernel(tq, n_toep, kk, cb_ref, qT_ref, qiT_ref, wiT_ref, kh_ref, vT_ref, ki_ref, toep_ref,
                        o_ref, keys_ref, m_ref, l_ref, acc_ref):
    tk = tq
    qt = pl.program_id(1)
    nk = qt + 1
    krow = lax.broadcasted_iota(I32, (tk, tq), 0)
    qcol = lax.broadcasted_iota(I32, (tk, tq), 1)

    def score_body(kt, carry):
        kic = ki_ref[0, kt]
        sc = jnp.zeros((tk, tq), F32)
        for h in range(IDX_HEADS):
            s = _dot(kic, qiT_ref[0, 0, h])
            sc = sc + jnp.maximum(s, 0.0) * (wiT_ref[0, h:h + 1, :] * IDX_HEADS ** -0.5)
        causal = (kt * tk + krow) <= (qt * tq + qcol)
        keys_ref[kt] = jnp.where(causal, _monotone_key(sc), INT_MIN)
        return carry

    lax.fori_loop(0, nk, score_body, 0)

    def count_ge(cand):
        def cnt_body(kt, c):
            hit = jnp.where(keys_ref[kt] >= cand, 1, 0)
            return c + jnp.sum(hit.reshape(tk // SUBLANES, SUBLANES, tq), axis=0)
        c = lax.fori_loop(0, nk, cnt_body, jnp.zeros((SUBLANES, tq), I32))
        return jnp.sum(c, axis=0, keepdims=True)

    thr = jnp.maximum(_topk_threshold(count_ge, (1, tq), kk), INT_MIN + 1)

    m_ref[...] = jnp.full(m_ref.shape, NEG, F32)
    l_ref[...] = jnp.zeros(l_ref.shape, F32)
    acc_ref[...] = jnp.zeros(acc_ref.shape, F32)

    def kv_step(kt, bias_of, thr_k):
        mask = keys_ref[kt] >= thr_k
        for h in range(N_HEADS):
            s = _dot(kh_ref[0, h, kt], qT_ref[0, 0, h]) + bias_of(h)
            m_old = m_ref[h]
            m_new = jnp.maximum(m_old, jnp.max(jnp.where(mask, s, NEG), axis=0, keepdims=True))
            p = jnp.where(mask, jnp.exp(s - m_new), 0.0)
            alpha = jnp.exp(m_old - m_new)
            l_ref[h] = alpha * l_ref[h] + jnp.sum(p, axis=0, keepdims=True)
            acc_ref[h] = alpha * acc_ref[h] + _dot(vT_ref[0, kt, h], p.astype(BF))
            m_ref[h] = m_new

    def far_body(kt, carry):
        kv_step(kt, lambda h: cb_ref[h], thr)
        return carry

    lax.fori_loop(0, jnp.maximum(nk - n_toep, 0), far_body, 0)
    for o in reversed(range(n_toep)):
        thr_o = jnp.where(qt >= o, thr, INT_MAX)
        kv_step(jnp.maximum(qt - o, 0), lambda h, o=o: toep_ref[o, h], thr_o)
    for h in range(N_HEADS):
        o_ref[0, h] = acc_ref[h] / l_ref[h]


def _attn_prompt(cb, qT, qiT, wiT, kh, vT, ki, toep, tq, kk):
    b, nq = qT.shape[:2]
    s = nq * tq
    n_toep = toep.shape[0]
    q_tile = lambda n: pl.BlockSpec((1, 1, N_HEADS, n, tq), lambda i, j: (i, j, 0, 0, 0))
    return pl.pallas_call(
        functools.partial(_attn_prompt_kernel, tq, n_toep, kk),
        grid=(b, nq),
        in_specs=[pl.BlockSpec(memory_space=pltpu.SMEM),
                  q_tile(HEAD_DIM), q_tile(IDX_DIM),
                  pl.BlockSpec((1, IDX_HEADS, tq), lambda i, j: (i, 0, j)),
                  pl.BlockSpec((1, N_HEADS, nq, tq, HEAD_DIM), lambda i, j: (i, 0, 0, 0, 0)),
                  pl.BlockSpec((1, nq, N_HEADS, HEAD_DIM, tq), lambda i, j: (i, 0, 0, 0, 0)),
                  pl.BlockSpec((1, nq, tq, IDX_DIM), lambda i, j: (i, 0, 0, 0)),
                  _full(toep.shape)],
        out_specs=pl.BlockSpec((1, N_HEADS, HEAD_DIM, tq), lambda i, j: (i, 0, 0, j)),
        out_shape=jax.ShapeDtypeStruct((b, N_HEADS, HEAD_DIM, s), F32),
        scratch_shapes=[pltpu.VMEM((nq, tq, tq), I32),
                        pltpu.VMEM((N_HEADS, 1, tq), F32), pltpu.VMEM((N_HEADS, 1, tq), F32),
                        pltpu.VMEM((N_HEADS, HEAD_DIM, tq), F32)],
        compiler_params=_params(2),
        name="attn_prompt",
    )(cb, qT, qiT, wiT, kh, vT, ki, toep)


PAGES_PER_STEP = 16


def _page_specs(block, n_pages, per_step=PAGES_PER_STEP):
    def spec(r):
        return pl.BlockSpec(block, lambda b, g, pt: (0, pt[b * n_pages + g * per_step + r]) + (0,) * (len(block) - 2))
    return [spec(r) for r in range(per_step)]


def _samp_scores_kernel(pt_ref, qi_ref, wi_ref, *refs):
    pages, o_ref = refs[:-1], refs[-1]
    qi = qi_ref[0]
    w = wi_ref[0] * IDX_HEADS ** -0.5
    rows = []
    for page in pages:
        s = _dot(qi, page[0, 0].astype(BF))
        rows.append(jnp.sum(jnp.maximum(s, 0.0) * w, axis=0, keepdims=True))
    o_ref[0] = jnp.concatenate(rows, axis=0)


def _samp_scores(page_table_flat, qi, wi, cache_idx_t, n_pages):
    db = qi.shape[0]
    grid_spec = pltpu.PrefetchScalarGridSpec(
        num_scalar_prefetch=1,
        grid=(db, 1),
        in_specs=[pl.BlockSpec((1, IDX_HEADS, IDX_DIM), lambda b, g, pt: (b, 0, 0)),
                  pl.BlockSpec((1, IDX_HEADS, 1), lambda b, g, pt: (b, 0, 0))]
                 + _page_specs((1, 1, IDX_DIM, PAGE_SIZE), n_pages, n_pages),
        out_specs=pl.BlockSpec((1, n_pages, PAGE_SIZE), lambda b, g, pt: (b, 0, 0)),
    )
    return pl.pallas_call(
        _samp_scores_kernel,
        grid_spec=grid_spec,
        out_shape=jax.ShapeDtypeStruct((db, n_pages, PAGE_SIZE), F32),
        compiler_params=_params(2),
        name="samp_scores",
    )(page_table_flat, qi, wi, *([cache_idx_t] * n_pages))


SEQ_PER_STEP = 8


def _total(x):
    return jnp.sum(jnp.sum(x, axis=0, keepdims=True), axis=1, keepdims=True)


def _ind(m):
    return jnp.where(m, 1.0, 0.0)


def _samp_select_kernel(kk, sc_ref, qi_ref, ki8_ref, kw_ref, tri_ref, lt_ref, sel_ref, nsel_ref,
                        keys_ref, thr_ref, knew_ref):
    key_new = []
    for r in range(SEQ_PER_STEP):
        prod = qi_ref[r].astype(F32) * ki8_ref[r].astype(F32)
        kwv = kw_ref[r]
        s_new = jnp.zeros((1, 1), F32)
        for h in range(IDX_HEADS):
            s = jnp.sum(prod[:, h * IDX_DIM:(h + 1) * IDX_DIM], axis=1, keepdims=True)
            s_new = s_new + jnp.maximum(s, 0.0) * (kwv[:, IDX_DIM + h:IDX_DIM + h + 1] * IDX_HEADS ** -0.5)
        key_new.append(_monotone_key(s_new))
        keys_ref[r] = _monotone_key(sc_ref[r])

    def bit_body(bi, thr_us):
        bit = jnp.left_shift(jnp.int32(1), 31 - bi)
        out = []
        for r in range(SEQ_PER_STEP):
            cand_u = thr_us[r] | bit
            cand = cand_u ^ INT_MIN
            cnt = _total(jnp.where(keys_ref[r] >= cand, 1, 0)) + jnp.where(key_new[r] >= cand, 1, 0)
            out.append(jnp.where(cnt >= kk, cand_u, thr_us[r]))
        return tuple(out)

    thr_us = lax.fori_loop(0, 32, bit_body, tuple(jnp.zeros((1, 1), I32) for _ in range(SEQ_PER_STEP)))
    for r in range(SEQ_PER_STEP):
        thr_ref[r] = jnp.broadcast_to(thr_us[r] ^ INT_MIN, (1, LANES))
        knew_ref[r] = jnp.broadcast_to(key_new[r], (1, LANES))

    def prefix(x):
        incl = _dot(x.astype(BF), tri_ref[...])
        page_tot = jnp.broadcast_to(incl[:, PAGE_SIZE - 1:PAGE_SIZE], incl.shape)
        return incl - x + _dot(lt_ref[...], page_tot.astype(BF))

    def seq_body(r, carry):
        keys = keys_ref[r]
        thr = thr_ref[r][:, :1]
        kn = knew_ref[r][:, :1]
        gt = keys > thr
        eq = keys == thr
        eq_f = _ind(eq)
        need = kk - (_total(_ind(gt)) + _ind(kn > thr))
        sel = jnp.where(gt, 1.0, jnp.where(eq, _ind(prefix(eq_f) < need), 0.0))
        new_sel = jnp.where(kn > thr, 1.0, jnp.where(kn == thr, _ind(_total(eq_f) < need), 0.0))
        sel_ref[r] = sel
        nsel_ref[r] = jnp.broadcast_to(new_sel, (1, LANES))
        return carry

    lax.fori_loop(0, SEQ_PER_STEP, seq_body, 0)


def _samp_select(scores, qib, ki8, kw, kk):
    db, n_pages, _ = scores.shape
    spb = SEQ_PER_STEP
    assert db % spb == 0
    pos = np.arange(PAGE_SIZE)
    tri = jnp.asarray(pos[:, None] <= pos[None, :], BF)
    pg = np.arange(n_pages)
    lt = jnp.asarray(pg[None, :] < pg[:, None], BF)
    per_seq = lambda n: pl.BlockSpec((spb, 1, n), lambda i: (i, 0, 0))
    return pl.pallas_call(
        functools.partial(_samp_select_kernel, kk),
        grid=(db // spb,),
        in_specs=[pl.BlockSpec((spb, n_pages, PAGE_SIZE), lambda i: (i, 0, 0)),
                  per_seq(qib.shape[2]), per_seq(ki8.shape[2]), per_seq(kw.shape[2]),
                  _full(tri.shape), _full(lt.shape)],
        out_specs=[pl.BlockSpec((spb, n_pages, PAGE_SIZE), lambda i: (i, 0, 0)), per_seq(LANES)],
        out_shape=[jax.ShapeDtypeStruct((db, n_pages, PAGE_SIZE), F32),
                   jax.ShapeDtypeStruct((db, 1, LANES), F32)],
        scratch_shapes=[pltpu.VMEM((spb, n_pages, PAGE_SIZE), I32),
                        pltpu.VMEM((spb, 1, LANES), I32), pltpu.VMEM((spb, 1, LANES), I32)],
        compiler_params=_params(1),
        name="samp_select",
    )(scores, qib, ki8, kw, tri, lt)


def _samp_attn_kernel(pt_ref, q_ref, qt_ref, sel_ref, bias_ref, bnew_ref, kn_ref, vnt_ref, *refs):
    k_pages = refs[:PAGES_PER_STEP]
    v_pages = refs[PAGES_PER_STEP:2 * PAGES_PER_STEP]
    o_ref, m_ref, l_ref, acc_ref, qb_ref = refs[2 * PAGES_PER_STEP:]
    g = pl.program_id(1)

    @pl.when(g == 0)
    def _():
        m_ref[...] = jnp.full(m_ref.shape, NEG, F32)
        l_ref[...] = jnp.zeros(l_ref.shape, F32)
        acc_ref[...] = jnp.zeros(acc_ref.shape, F32)
        for h in range(N_HEADS):
            qb_ref[h] = jnp.broadcast_to(qt_ref[0][:, h:h + 1], (HEAD_DIM, PAGE_SIZE))

    logits, masks = [], []
    for r in range(PAGES_PER_STEP):
        rows = [jnp.sum(k_pages[r][0, 0, h] * qb_ref[h], axis=0, keepdims=True) for h in range(N_HEADS)]
        mask = sel_ref[0, r:r + 1, :] > 0.5
        masks.append(mask)
        logits.append(jnp.where(mask, jnp.concatenate(rows, axis=0) + bias_ref[0, r], NEG))
    m_old = m_ref[...]
    m_new = m_old
    for s in logits:
        m_new = jnp.maximum(m_new, jnp.max(s, axis=1, keepdims=True))
    alpha = jnp.exp(m_old - m_new)
    probs = [jnp.where(mask, jnp.exp(s - m_new), 0.0) for s, mask in zip(logits, masks)]
    l = alpha * l_ref[...]
    for p in probs:
        l = l + jnp.sum(p, axis=1, keepdims=True)
    l_ref[...] = l
    m_ref[...] = m_new
    for h in range(N_HEADS):
        acc = acc_ref[h] * alpha[h:h + 1, :]
        for r in range(PAGES_PER_STEP):
            acc = acc + v_pages[r][0, 0, h] * probs[r][h:h + 1, :]
        acc_ref[h] = acc

    @pl.when(g == pl.num_programs(1) - 1)
    def _():
        s_new = jnp.sum(kn_ref[0] * q_ref[0], axis=1, keepdims=True) + bnew_ref[0]
        m_fin = jnp.maximum(m_new, s_new)
        a_fin = jnp.exp(m_new - m_fin)
        p_new = jnp.exp(s_new - m_fin)
        l_fin = a_fin * l + p_new
        cols = []
        for h in range(N_HEADS):
            past = jnp.sum(acc_ref[h], axis=1, keepdims=True) * a_fin[h:h + 1, :]
            cols.append((past + vnt_ref[0][:, h:h + 1] * p_new[h:h + 1, :]) / l_fin[h:h + 1, :])
        o_ref[0] = jnp.concatenate(cols, axis=1)


def _samp_attn(page_table_flat, q, qt, sel, bias_t, bias_new, kn, vnt, cache_k_t, cache_v_t, n_pages):
    db = q.shape[0]
    ng = n_pages // PAGES_PER_STEP
    per_seq = lambda shape: pl.BlockSpec((1,) + shape, lambda b, g, pt: (b,) + (0,) * len(shape))
    page_block = (1, 1, N_HEADS, HEAD_DIM, PAGE_SIZE)
    grid_spec = pltpu.PrefetchScalarGridSpec(
        num_scalar_prefetch=1,
        grid=(db, ng),
        in_specs=[per_seq((N_HEADS, HEAD_DIM)), per_seq((HEAD_DIM, N_HEADS)),
                  pl.BlockSpec((1, PAGES_PER_STEP, PAGE_SIZE), lambda b, g, pt: (b, g, 0)),
                  pl.BlockSpec((1, PAGES_PER_STEP, N_HEADS, PAGE_SIZE), lambda b, g, pt: (g, 0, 0, 0)),
                  per_seq((N_HEADS, 1)), per_seq((N_HEADS, HEAD_DIM)), per_seq((HEAD_DIM, N_HEADS))]
                 + _page_specs(page_block, n_pages) + _page_specs(page_block, n_pages),
        out_specs=per_seq((HEAD_DIM, N_HEADS)),
        scratch_shapes=[pltpu.VMEM((N_HEADS, 1), F32), pltpu.VMEM((N_HEADS, 1), F32),
                        pltpu.VMEM((N_HEADS, HEAD_DIM, PAGE_SIZE), F32),
                        pltpu.VMEM((N_HEADS, HEAD_DIM, PAGE_SIZE), F32)],
    )
    return pl.pallas_call(
        _samp_attn_kernel,
        grid_spec=grid_spec,
        out_shape=jax.ShapeDtypeStruct((db, HEAD_DIM, N_HEADS), F32),
        compiler_params=_params(2),
        name="samp_attn",
    )(page_table_flat, q, qt, sel, bias_t, bias_new, kn, vnt,
      *([cache_k_t] * PAGES_PER_STEP), *([cache_v_t] * PAGES_PER_STEP))


def _merge_kernel(per_row, x_ref, yc_ref, at_ref, sgc_ref, sga_ref, mod_ref, lng_ref, lnb_ref,
                  wco_ref, wao_ref, wout_ref, g2_ref, wpq_ref, kbd_ref,
                  xm_ref, h2t_ref, st_ref):
    yc = yc_ref[...]
    mu = jnp.mean(yc, axis=-1, keepdims=True)
    yd = yc - mu
    ln = yd * lax.rsqrt(jnp.mean(yd * yd, axis=-1, keepdims=True) + EPS) * lng_ref[...] + lnb_ref[...]
    act = ln * jax.nn.sigmoid(ln)
    conv_out = _dot(act.astype(BF), wco_ref[...])
    attn_out = _dot(at_ref[0].T.astype(BF), wao_ref[...])
    merged = sgc_ref[...] * conv_out + sga_ref[...] * attn_out
    xm = x_ref[...] + _mod(mod_ref, 2, per_row) * _dot(merged.astype(BF), wout_ref[...])
    xm_ref[...] = xm
    rs = lax.rsqrt(jnp.mean(xm * xm, axis=-1, keepdims=True) + EPS)
    h2 = (xm * rs) * g2_ref[...] * (1.0 + _mod(mod_ref, 4, per_row)) + _mod(mod_ref, 3, per_row)
    h2t_ref[...] = pltpu.bitcast(h2.T.astype(BF), I32)
    pq = _dot(h2.astype(BF), wpq_ref[...])
    st_ref[...] = _dot_nt(kbd_ref[...], pq.astype(BF))


def _merge(x, yc, at, sgc, sga, mod, per_row, rows_per_seq, tm, wts):
    t = x.shape[0]
    row = lambda n: pl.BlockSpec((tm, n), lambda i: (i, 0))
    col = lambda n: pl.BlockSpec((n, tm), lambda i: (0, i))
    n_scores = PEER_HEADS * 2 * PEER_KEYS
    return pl.pallas_call(
        functools.partial(_merge_kernel, per_row),
        grid=(t // tm,),
        in_specs=[row(D_MODEL), row(CONV_DIM),
                  pl.BlockSpec((1, ATTN_DIM, tm), lambda i: (i // (at.shape[2] // tm), 0, i % (at.shape[2] // tm))),
                  row(D_MODEL), row(D_MODEL),
                  _mod_spec(per_row, tm, rows_per_seq)] + [_full(w.shape) for w in wts],
        out_specs=[row(D_MODEL), col(D_MODEL // 2), col(n_scores)],
        out_shape=[jax.ShapeDtypeStruct((t, D_MODEL), F32),
                   jax.ShapeDtypeStruct((D_MODEL // 2, t), I32),
                   jax.ShapeDtypeStruct((n_scores, t), F32)],
        compiler_params=_params(1),
        name="merge",
    )(x, yc, at, sgc, sga, mod, *wts)


N_EXTRACT = PEER_TOPK + 1


def _top_extract(cur, n):
    outs = []
    for _ in range(n):
        mx = jnp.max(cur, axis=0, keepdims=True)
        outs.append(mx)
        cur = jnp.where(cur >= mx, -jnp.inf, cur)
    return outs


RANK_NONE = float(PEER_KEYS - 1)


def _peer_select_kernel(lb, st_ref, r1_ref, e1_ref, n_ref, al_ref):
    tl = st_ref.shape[1]

    def head_body(h, carry):
        base = pl.multiple_of(h * 2 * PEER_KEYS, 2 * PEER_KEYS)
        s0 = st_ref[pl.ds(base, PEER_KEYS), :]
        s1 = st_ref[pl.ds(base + PEER_KEYS, PEER_KEYS), :]
        a0 = _top_extract(s0, N_EXTRACT)
        a1 = []
        cur = s1
        rank1 = jnp.full(s1.shape, RANK_NONE, F32)
        for k in range(N_EXTRACT):
            mx = jnp.max(cur, axis=0, keepdims=True)
            a1.append(mx)
            hit = cur >= mx
            if k < PEER_TOPK:
                rank1 = jnp.where(hit, float(k), rank1)
            cur = jnp.where(hit, -jnp.inf, cur)
        pad = [jnp.full((1, tl), -jnp.inf, F32)] * (3 * SUBLANES - N_EXTRACT)
        a1_all = jnp.concatenate(a1 + pad, axis=0)
        cands = [a0[0] + a1_all] + [a0[k] + a1_all[:SUBLANES] for k in range(1, N_EXTRACT)]
        top = _top_extract(jnp.concatenate(cands, axis=0), N_EXTRACT)
        thr = 0.5 * (top[PEER_TOPK - 1] + top[PEER_TOPK])
        z = jnp.zeros_like(thr)
        for k in range(PEER_TOPK):
            z = z + jnp.exp(top[k] - top[0])
        tau = thr - s0
        cnt = jnp.zeros(s0.shape, F32)
        for l in range(PEER_TOPK):
            cnt = cnt + jnp.where(a1[l] >= tau, 1.0, 0.0)
        e1 = jnp.exp(s1 - a1[0])
        al = jnp.exp(s0 - a0[0]) / z
        for tb in range(tl // lb):
            ts = slice(tb * lb, (tb + 1) * lb)
            r1_ref[h, tb] = rank1[:, ts]
            e1_ref[h, tb] = e1[:, ts]
            n_ref[h, tb] = cnt[:, ts]
            al_ref[h, tb] = al[:, ts]
        return carry

    lax.fori_loop(0, PEER_HEADS, head_body, 0)


def _peer_select(st, tl):
    t = st.shape[1]
    lb = min(LANES, t)
    spec = pl.BlockSpec((PEER_HEADS, tl // lb, PEER_KEYS, lb), lambda i: (0, i, 0, 0))
    shp = jax.ShapeDtypeStruct((PEER_HEADS, t // lb, PEER_KEYS, lb), F32)
    return pl.pallas_call(
        functools.partial(_peer_select_kernel, lb),
        grid=(t // tl,),
        in_specs=[pl.BlockSpec((st.shape[0], tl), lambda i: (0, i))],
        out_specs=[spec] * 4,
        out_shape=[shp] * 4,
        compiler_params=_params(1),
        name="peer_select",
    )(st)


PEER_PAIRS_PER_STEP = SUBLANES


def _peer_main_kernel(per_row, h2t_ref, *refs):
    n_units = PEER_PAIRS_PER_STEP // 2
    u_refs = (refs[:n_units], refs[n_units:2 * n_units])
    vt_refs = (refs[2 * n_units:3 * n_units], refs[3 * n_units:4 * n_units])
    r1_ref, e1_ref, n_ref, al_ref, xm_ref, mod_ref, o_ref, acc_ref, a_ref, g_ref, w_ref = refs[4 * n_units:]
    j = pl.program_id(1)
    tm = h2t_ref.shape[1]
    n_tb, _, lb = r1_ref.shape[1:]

    @pl.when(j == 0)
    def _():
        acc_ref[...] = jnp.zeros(acc_ref.shape, F32)

    i0 = pl.multiple_of(j * PEER_PAIRS_PER_STEP, PEER_PAIRS_PER_STEP)
    h2t = pltpu.bitcast(h2t_ref[...], BF)
    unit = 2 * PEER_KEYS
    jn = PEER_KEYS // 4

    def activations(un):
        for c in range(2):
            a_ref[un % 2, c] = _dot(pltpu.bitcast(u_refs[c][un][...], BF), h2t)

    def weights(un):
        for tb in range(n_tb):
            ts = slice(tb * lb, (tb + 1) * lb)
            for jb in range(PEER_KEYS // jn):
                js = slice(jb * jn, (jb + 1) * jn)
                w = [[jnp.zeros((jn, lb), F32) for _ in range(2)] for _ in range(2)]
                for h in range(PEER_HEADS):
                    groups = [pl.ds(pl.multiple_of(i0 + c * (PEER_KEYS // 2), PEER_PAIRS_PER_STEP),
                                    PEER_PAIRS_PER_STEP) for c in range(2)]
                    cnt = [n_ref[h, tb, grp, :] for grp in groups]
                    al = [al_ref[h, tb, grp, :] for grp in groups]
                    r1 = r1_ref[h, tb, js, :]
                    e1 = e1_ref[h, tb, js, :]
                    for c in range(2):
                        for q in range(2):
                            pp = 2 * un + q
                            w[c][q] = w[c][q] + jnp.where(r1 < cnt[c][pp:pp + 1], e1 * al[c][pp:pp + 1], 0.0)
                for c in range(2):
                    for q in range(2):
                        lo = q * PEER_KEYS + jb * jn
                        w_ref[c, lo:lo + jn, ts] = w[c][q]

    activations(0)
    for un in range(n_units):
        rows = slice(un * unit, (un + 1) * unit)
        weights(un)
        if un + 1 < n_units:
            activations(un + 1)
        for c in range(2):
            a = a_ref[un % 2, c]
            gelu = 0.5 * a * (1.0 + lax.erf(a * (2.0 ** -0.5)))
            g_ref[c, rows, :] = (w_ref[c] * gelu).astype(BF)
        acc_ref[...] += (_dot(pltpu.bitcast(vt_refs[0][un][...], BF), g_ref[0, rows, :])
                         + _dot(pltpu.bitcast(vt_refs[1][un][...], BF), g_ref[1, rows, :]))

    @pl.when(j == pl.num_programs(1) - 1)
    def _():
        o_ref[...] = xm_ref[...] + _mod(mod_ref, 5, per_row) * acc_ref[...].T


def _peer_main(h2t, u_pk, vt_pk, r1, e1, cnt, al, xm, mod, per_row, rows_per_seq, tm):
    t = xm.shape[0]
    ec = PEER_PAIRS_PER_STEP * PEER_KEYS
    n_steps = u_pk.shape[0] * 2 // (2 * ec)
    n_units = PEER_PAIRS_PER_STEP // 2
    unit = 2 * PEER_KEYS
    units_lo = range(n_units)
    units_hi = range(n_steps * n_units, (n_steps + 1) * n_units)
    lb = r1.shape[3]
    sel = lambda rows: pl.BlockSpec((PEER_HEADS, tm // lb, rows, lb), lambda i, j: (0, i, 0, 0))
    if per_row:
        mod_spec = pl.BlockSpec((6, tm, D_MODEL), lambda i, j: (0, i, 0))
    else:
        tiles_per_seq = rows_per_seq // tm
        mod_spec = pl.BlockSpec((1, 6, D_MODEL), lambda i, j: (i // tiles_per_seq, 0, 0))
    return pl.pallas_call(
        functools.partial(_peer_main_kernel, per_row),
        grid=(t // tm, n_steps),
        in_specs=[pl.BlockSpec((D_MODEL // 2, tm), lambda i, j: (0, i))]
                 + [pl.BlockSpec((unit // 2, D_MODEL), lambda i, j, k=k: (j * n_units + k, 0)) for k in units_lo]
                 + [pl.BlockSpec((unit // 2, D_MODEL), lambda i, j, k=k: (j * n_units + k, 0)) for k in units_hi]
                 + [pl.BlockSpec((D_MODEL // 2, unit), lambda i, j, k=k: (0, j * n_units + k)) for k in units_lo]
                 + [pl.BlockSpec((D_MODEL // 2, unit), lambda i, j, k=k: (0, j * n_units + k)) for k in units_hi]
                 + [sel(PEER_KEYS), sel(PEER_KEYS), sel(PEER_KEYS), sel(PEER_KEYS),
                  pl.BlockSpec((tm, D_MODEL), lambda i, j: (i, 0)),
                  mod_spec],
        out_specs=pl.BlockSpec((tm, D_MODEL), lambda i, j: (i, 0)),
        out_shape=jax.ShapeDtypeStruct((t, D_MODEL), F32),
        scratch_shapes=[pltpu.VMEM((D_MODEL, tm), F32), pltpu.VMEM((2, 2, 2 * PEER_KEYS, tm), F32),
                        pltpu.VMEM((2, ec, tm), BF), pltpu.VMEM((2, 2 * PEER_KEYS, tm), F32)],
        compiler_params=_params(2),
        name="peer_main",
    )(h2t, *([u_pk] * (2 * n_units)), *([vt_pk] * (2 * n_units)), r1, e1, cnt, al, xm, mod)


PACK_ROWS = 512


def _peer_pack_kernel(u_ref, v_ref, u_pk_ref, vt_pk_ref):
    u_pk_ref[...] = pltpu.bitcast(u_ref[...].astype(BF), I32)
    vt_pk_ref[...] = pltpu.bitcast(v_ref[...].T.astype(BF), I32)


def _peer_pack(u, v):
    n_e = u.shape[0]
    return pl.pallas_call(
        _peer_pack_kernel,
        grid=(n_e // PACK_ROWS,),
        in_specs=[pl.BlockSpec((PACK_ROWS, D_MODEL), lambda i: (i, 0))] * 2,
        out_specs=[pl.BlockSpec((PACK_ROWS // 2, D_MODEL), lambda i: (i, 0)),
                   pl.BlockSpec((D_MODEL // 2, PACK_ROWS), lambda i: (0, i))],
        out_shape=[jax.ShapeDtypeStruct((n_e // 2, D_MODEL), I32),
                   jax.ShapeDtypeStruct((D_MODEL // 2, n_e), I32)],
        compiler_params=_params(1),
        name="peer_pack",
    )(u, v)


def _tile(n, pref):
    return pref if n % pref == 0 else n


def kernel(x_prompt, x_sample, cache_k, cache_v, cache_idx_k, state_conv, page_table, c_prompt, c_sample,
           w_ada, b_ada, norm1_g, w_in, q_norm_g, k_norm_g, rel_bias, w_dw, b_dw, conv_ln_g, conv_ln_b,
           w_conv_o, w_attn_o, w_out, norm2_g, w_peer_q, peer_keys, peer_u, peer_v):
    b, s, _ = x_prompt.shape
    db, ds, _ = x_sample.shape
    n_pages = page_table.shape[1]
    past = n_pages * PAGE_SIZE
    assert w_ada.shape[0] == 1, "one layer"
    assert ds == 1, "one new token per sample sequence"
    assert n_pages % PAGES_PER_STEP == 0
    tp = b * s

    w = w_in[0]
    pts = np.cumsum([0, 2 * CONV_DIM, ATTN_DIM, ATTN_DIM, ATTN_DIM, IDX_HEADS * IDX_DIM, IDX_DIM, IDX_HEADS,
                     D_MODEL, D_MODEL])
    seg = lambda i: w[:, pts[i]:pts[i + 1]].astype(BF)
    w_kw = jnp.pad(w[:, pts[5]:pts[7]], ((0, 0), (0, LANES - IDX_DIM - IDX_HEADS))).astype(BF)
    head_of = np.arange(ATTN_DIM) // HEAD_DIM
    bd = jnp.asarray(head_of[:, None] == head_of[None, :], BF)
    row = lambda v: v.reshape(1, -1)
    in_wts = [row(norm1_g[0]), seg(0), seg(1), seg(2), seg(3), seg(4), w_kw, seg(7), seg(8),
              row(jnp.tile(q_norm_g[0], N_HEADS)), row(jnp.tile(k_norm_g[0], N_HEADS)), bd]
    n_half = PEER_HEADS * 2
    kb16 = peer_keys[0].reshape(n_half, PEER_KEYS, PEER_HALF)
    kbd_t = (jnp.eye(n_half, dtype=F32)[:, None, :, None] * kb16[:, :, None, :]).reshape(
        n_half * PEER_KEYS, n_half * PEER_HALF).astype(BF)
    merge_wts = [row(conv_ln_g[0]), row(conv_ln_b[0]), w_conv_o[0].astype(BF), w_attn_o[0].astype(BF),
                 w_out[0].astype(BF), row(norm2_g[0]), w_peer_q[0].astype(BF), kbd_t]
    u_pk, vt_pk = _peer_pack(peer_u[0], peer_v[0])

    mod = _ada(jnp.concatenate([c_prompt, c_sample], axis=0), w_ada[0].astype(BF), row(b_ada[0]))
    mod_p = mod[:b].reshape(b, 6, D_MODEL)
    mod_s = mod[b:].reshape(db, 6, D_MODEL).transpose(1, 0, 2)

    tq = _tile(s, 256)
    nq = s // tq
    glu_p, sgc_p, sga_p, kT_p, vT32_p, kiT_p, qT_p, qiT_p, vT_p, kh_p, ki_p, wiT_p = _inproj(
        x_prompt.reshape(tp, D_MODEL), mod_p, False, s, tq, in_wts)
    glu_s, sgc_s, sga_s, k_s, v_s, kw_s, qb_s, qib_s = _inproj(
        x_sample.reshape(db, D_MODEL), mod_s, True, 1, db, in_wts)

    yc_p = _conv_prompt(glu_p.reshape(b, s, CONV_DIM), w_dw[0], row(b_dw[0]), _tile(s, 256)).reshape(tp, CONV_DIM)
    yc_s = _conv_sample(state_conv[0].transpose(1, 0, 2), glu_s, w_dw[0], row(b_dw[0]))

    bucket =_rel_bucket_np(np.arange(max(s, past + 1)))
    n_toep = nq
    while n_toep > 0 and len(set(bucket[max((n_toep - 1) * tq - (tq - 1), 0):n_toep * tq])) == 1 \
            and bucket[(n_toep - 1) * tq] == bucket[s - 1]:
        n_toep -= 1
    n_toep = max(n_toep, 1)

    def bias_of_bucket(bkt):
        onehot = jnp.asarray(np.eye(REL_BUCKETS, dtype=np.float32)[bkt.reshape(-1)])
        return jnp.dot(onehot, rel_bias, precision=lax.Precision.HIGHEST).reshape(bkt.shape + (N_HEADS,))

    width = 2 * tq + 1
    dist = (np.arange(n_toep)[:, None] - 1) * tq + np.arange(width)[None, :]
    table = bias_of_bucket(_rel_bucket_np(dist)).transpose(0, 2, 1)
    skew = jnp.tile(table, (1, 1, tq))[:, :, :tq * (width - 1)].reshape(n_toep, N_HEADS, tq, width - 1)
    toep = skew[:, :, :, tq:]
    cb = rel_bias[int(bucket[s - 1])]

    kk_p = min(IDX_TOPK_MAX, s // 4)
    split_heads = lambda a: a.reshape(b, nq, N_HEADS, -1, tq)
    at_p = _attn_prompt(cb, split_heads(qT_p), split_heads(qiT_p), wiT_p, kh_p, split_heads(vT_p), ki_p,
                        toep, tq, kk_p)
    at_p = at_p.reshape(b, ATTN_DIM, s)

    kk_s = min(IDX_TOPK_MAX, (past + ds) // 4)
    pt_flat = page_table.reshape(-1)
    sc = _samp_scores(pt_flat, qib_s.reshape(db, IDX_HEADS, IDX_DIM),
                      kw_s[:, IDX_DIM:IDX_DIM + IDX_HEADS].reshape(db, IDX_HEADS, 1),
                      cache_idx_k.transpose(0, 1, 3, 2), n_pages)
    ki8 = jnp.tile(kw_s[:, :IDX_DIM].astype(BF), (1, IDX_HEADS))
    sel_s, new_sel = _samp_select(sc.reshape(db, n_pages, PAGE_SIZE), qib_s[:, None, :], ki8[:, None, :],
                                  kw_s[:, None, :], kk_s)
    pos = np.arange(past).reshape(n_pages // PAGES_PER_STEP, PAGES_PER_STEP, PAGE_SIZE)
    bias_t = bias_of_bucket(_rel_bucket_np(past - pos)).transpose(0, 1, 3, 2)
    bias_new = jnp.where(new_sel[:, 0, :1] > 0.0, rel_bias[int(_rel_bucket_np(np.zeros((), np.int64)))][None], NEG)
    heads = lambda a: a.reshape(db, N_HEADS, HEAD_DIM)
    q_s = heads(qb_s.astype(F32))
    at_s = _samp_attn(pt_flat, q_s, q_s.transpose(0, 2, 1), sel_s, bias_t, bias_new.reshape(db, N_HEADS, 1),
                      heads(k_s), heads(v_s).transpose(0, 2, 1),
                      cache_k.transpose(0, 1, 3, 4, 2), cache_v.transpose(0, 1, 3, 4, 2), n_pages)
    at_s = at_s.transpose(2, 1, 0).reshape(1, ATTN_DIM, db)

    def tail(x, yc, at, sgc, sga, mod_g, per_row, rows_per_seq, tm_merge, tl_sel, tm_peer):
        xm, h2t, st = _merge(x, yc, at, sgc, sga, mod_g, per_row, rows_per_seq, tm_merge, merge_wts)
        s1, e1, tau, al = _peer_select(st, tl_sel)
        return _peer_main(h2t, u_pk, vt_pk, s1, e1, tau, al, xm, mod_g, per_row, rows_per_seq, tm_peer)

    y_p = tail(x_prompt.reshape(tp, D_MODEL), yc_p, at_p, sgc_p, sga_p, mod_p, False, s,
               _tile(s, 256), _tile(tp, 256), _tile(s, 512))
    y_s = tail(x_sample.reshape(db, D_MODEL), yc_s, at_s, sgc_s, sga_s, mod_s, True, 1, db, db, db)

    glu_p3 = glu_p.reshape(b, s, CONV_DIM)
    conv_s = jnp.concatenate([state_conv[0], glu_s[:, None, :]], axis=1)[:, -CONV_STATE:]
    seq_major = lambda a: a.reshape(b, N_HEADS, HEAD_DIM, s).transpose(0, 3, 1, 2)[None]
    return (y_p.reshape(b, s, D_MODEL), y_s.reshape(db, ds, D_MODEL),
            seq_major(kT_p), seq_major(vT32_p),
            kiT_p.transpose(0, 2, 1)[None], glu_p3[None, :, -CONV_STATE:],
            k_s.reshape(1, db, ds, N_HEADS, HEAD_DIM), v_s.reshape(1, db, ds, N_HEADS, HEAD_DIM),
            kw_s[:, :IDX_DIM].reshape(1, db, ds, IDX_DIM), conv_s[None])
```

```python
import functools
import math

import numpy as np
import jax
import jax.numpy as jnp
from jax import lax
from jax.experimental import pallas as pl
from jax.experimental.pallas import tpu as pltpu

F32 = jnp.float32
BF = jnp.bfloat16
I32 = jnp.int32

D_MODEL = 1024
N_HEADS = 8
HEAD_DIM = 64
ATTN_DIM = N_HEADS * HEAD_DIM
IDX_HEADS = 8
IDX_DIM = 64
IDX_TOPK_MAX = 256
REL_BUCKETS = 32
REL_MAX_DIST = 128
CONV_DIM = 512
CONV_WIDTH = 31
CONV_STATE = CONV_WIDTH - 1
PEER_HEADS = 8
PEER_KEYS = 128
PEER_HALF = 64
PEER_TOPK = 16
PAGE_SIZE = 128
EPS = 1e-6

NEG = -1e30
INT_MIN = -(2 ** 31)
INT_MAX = 2 ** 31 - 1
LANES = 128
SUBLANES = 8
VMEM_LIMIT = 52 * 1024 * 1024


def _params(n_axes):
    return pltpu.CompilerParams(dimension_semantics=("arbitrary",) * n_axes,
                                vmem_limit_bytes=VMEM_LIMIT)


def _full(shape):
    zeros = (0,) * len(shape)
    return pl.BlockSpec(shape, lambda *_: zeros)


def _mod(mod_ref, k, per_row):
    return mod_ref[k] if per_row else mod_ref[0, k:k + 1, :]


def _mod_spec(per_row, tm, rows_per_seq):
    if per_row:
        return pl.BlockSpec((6, tm, D_MODEL), lambda i: (0, i, 0))
    tiles_per_seq = rows_per_seq // tm
    return pl.BlockSpec((1, 6, D_MODEL), lambda i: (i // tiles_per_seq, 0, 0))


def _dot(a, b):
    return jnp.dot(a, b, preferred_element_type=F32)


def _dot_nt(a, b):
    return lax.dot_general(a, b, (((1,), (1,)), ((), ())), preferred_element_type=F32)


def _monotone_key(x):
    bits = lax.bitcast_convert_type(x, I32)
    return bits ^ ((bits >> 31) & INT_MAX)


def _ada_kernel(c_ref, w_ref, b_ref, o_ref):
    c = c_ref[...]
    a = (c * jax.nn.sigmoid(c)).astype(BF)
    o_ref[...] = _dot(a, w_ref[...]) + b_ref[...]


def _ada(c, w_bf, b):
    rows = c.shape[0]
    n = w_bf.shape[1]
    nb = 1536
    return pl.pallas_call(
        _ada_kernel,
        grid=(n // nb,),
        in_specs=[_full((rows, D_MODEL)),
                  pl.BlockSpec((D_MODEL, nb), lambda j: (0, j)),
                  pl.BlockSpec((1, nb), lambda j: (0, j))],
        out_specs=pl.BlockSpec((rows, nb), lambda j: (0, j)),
        out_shape=jax.ShapeDtypeStruct((rows, n), F32),
        compiler_params=_params(1),
        name="ada",
    )(c, w_bf, b)


def _inproj_kernel(per_row, x_ref, mod_ref, g1_ref, wglu_ref, wq_ref, wk_ref, wv_ref, wqi_ref,
                   wkw_ref, wgc_ref, wga_ref, gq_ref, gk_ref, bd_ref,
                   glu_ref, sgc_ref, sga_ref, *attn_refs):
    x = x_ref[...]
    shift = _mod(mod_ref, 0, per_row)
    scale = _mod(mod_ref, 1, per_row)
    rs = lax.rsqrt(jnp.mean(x * x, axis=-1, keepdims=True) + EPS)
    h = (x * rs) * g1_ref[...] * (1.0 + scale) + shift
    hb = h.astype(BF)

    def head_norm(z, g_ref):
        sq = z * z
        hi = sq.astype(BF)
        lo = (sq - hi.astype(F32)).astype(BF)
        ss = _dot(hi, bd_ref[...]) + _dot(lo, bd_ref[...])
        return z * lax.rsqrt(ss * (1.0 / HEAD_DIM) + EPS) * g_ref[...]

    gl = _dot(hb, wglu_ref[...])
    glu_ref[...] = gl[:, :CONV_DIM] * jax.nn.sigmoid(gl[:, CONV_DIM:])
    q = head_norm(_dot(hb, wq_ref[...]), gq_ref)
    k = head_norm(_dot(hb, wk_ref[...]), gk_ref)
    v = _dot(hb, wv_ref[...])
    qs = q * HEAD_DIM ** -0.5
    qis = _dot(hb, wqi_ref[...]) * IDX_DIM ** -0.5
    kw = _dot(hb, wkw_ref[...])
    sgc_ref[...] = jax.nn.sigmoid(_dot(hb, wgc_ref[...]))
    sga_ref[...] = jax.nn.sigmoid(_dot(hb, wga_ref[...]))
    if per_row:
        k_ref, v_ref, kw_ref, qb_ref, qib_ref = attn_refs
        k_ref[...] = k
        v_ref[...] = v
        kw_ref[...] = kw
        qb_ref[...] = qs.astype(BF)
        qib_ref[...] = qis.astype(BF)
    else:
        kT_ref, vT_ref, kiT_ref, qTb_ref, qiTb_ref, vTb_ref, kh_ref, ki_ref, wiT_ref = attn_refs
        v_t = v.T
        kw_t = kw.T
        kT_ref[0] = k.T
        vT_ref[0] = v_t
        kiT_ref[0] = kw_t[:IDX_DIM, :]
        qTb_ref[0, 0] = qs.T.astype(BF)
        qiTb_ref[0, 0] = qis.T.astype(BF)
        vTb_ref[0, 0] = v_t.astype(BF)
        for hd in range(N_HEADS):
            kh_ref[0, hd, 0] = k[:, hd * HEAD_DIM:(hd + 1) * HEAD_DIM].astype(BF)
        ki_ref[0, 0] = kw[:, :IDX_DIM].astype(BF)
        wiT_ref[0] = kw_t[IDX_DIM:IDX_DIM + IDX_HEADS, :]


def _inproj(x, mod, per_row, rows_per_seq, tm, wts):
    t = x.shape[0]
    row = lambda n: pl.BlockSpec((tm, n), lambda i: (i, 0))
    rows = lambda n, dt: jax.ShapeDtypeStruct((t, n), dt)
    out_specs = [row(CONV_DIM), row(D_MODEL), row(D_MODEL)]
    out_shape = [rows(CONV_DIM, F32), rows(D_MODEL, F32), rows(D_MODEL, F32)]
    if per_row:
        out_specs += [row(ATTN_DIM), row(ATTN_DIM), row(LANES), row(ATTN_DIM), row(ATTN_DIM)]
        out_shape += [rows(ATTN_DIM, F32), rows(ATTN_DIM, F32), rows(LANES, F32), rows(ATTN_DIM, BF), rows(ATTN_DIM, BF)]
    else:
        nq = rows_per_seq // tm
        b = t // rows_per_seq
        seq_minor = lambda n: pl.BlockSpec((1, n, tm), lambda i: (i // nq, 0, i % nq))
        minor = pl.BlockSpec((1, 1, ATTN_DIM, tm), lambda i: (i // nq, i % nq, 0, 0))
        minor_shape = jax.ShapeDtypeStruct((b, nq, ATTN_DIM, tm), BF)
        out_specs += [seq_minor(ATTN_DIM), seq_minor(ATTN_DIM), seq_minor(IDX_DIM),
                      minor, minor, minor,
                      pl.BlockSpec((1, N_HEADS, 1, tm, HEAD_DIM), lambda i: (i // nq, 0, i % nq, 0, 0)),
                      pl.BlockSpec((1, 1, tm, IDX_DIM), lambda i: (i // nq, i % nq, 0, 0)),
                      seq_minor(IDX_HEADS)]
        out_shape += [jax.ShapeDtypeStruct((b, ATTN_DIM, rows_per_seq), F32),
                      jax.ShapeDtypeStruct((b, ATTN_DIM, rows_per_seq), F32),
                      jax.ShapeDtypeStruct((b, IDX_DIM, rows_per_seq), F32),
                      minor_shape, minor_shape, minor_shape,
                      jax.ShapeDtypeStruct((b, N_HEADS, nq, tm, HEAD_DIM), BF),
                      jax.ShapeDtypeStruct((b, nq, tm, IDX_DIM), BF),
                      jax.ShapeDtypeStruct((b, IDX_HEADS, rows_per_seq), F32)]
    return pl.pallas_call(
        functools.partial(_inproj_kernel, per_row),
        grid=(t // tm,),
        in_specs=[row(D_MODEL), _mod_spec(per_row, tm, rows_per_seq)] + [_full(w.shape) for w in wts],
        out_specs=out_specs,
        out_shape=out_shape,
        compiler_params=_params(1),
        name="inproj",
    )(x, mod, *wts)


CONV_HALO = 32


def _conv_prompt_kernel(tt, glu_ref, wdw_ref, bdw_ref, y_ref, ext_ref):
    j = pl.program_id(1)

    @pl.when(j == 0)
    def _():
        ext_ref[0:CONV_HALO, :] = jnp.zeros((CONV_HALO, CONV_DIM), F32)
        ext_ref[CONV_HALO:, :] = glu_ref[0]

    n = tt + CONV_HALO
    t0 = pl.multiple_of(j * tt, tt)
    win = ext_ref[pl.ds(t0, n), :]
    acc = jnp.zeros((tt, CONV_DIM), F32) + bdw_ref[...]
    lead = CONV_HALO - CONV_STATE
    for r in range(SUBLANES):
        rolled = win if r == 0 else pltpu.roll(win, n - r, axis=0)
        for tap in range(CONV_WIDTH):
            off = lead + tap
            if off % SUBLANES == r:
                a = off - r
                acc = acc + wdw_ref[tap:tap + 1, :] * rolled[a:a + tt]
    y_ref[0] = acc


def _conv_prompt(glu, wdw, bdw, tt):
    b, s, _ = glu.shape
    return pl.pallas_call(
        functools.partial(_conv_prompt_kernel, tt),
        grid=(b, s // tt),
        in_specs=[pl.BlockSpec((1, s, CONV_DIM), lambda i, j: (i, 0, 0)),
                  _full(wdw.shape), _full(bdw.shape)],
        out_specs=pl.BlockSpec((1, tt, CONV_DIM), lambda i, j: (i, j, 0)),
        out_shape=jax.ShapeDtypeStruct((b, s, CONV_DIM), F32),
        scratch_shapes=[pltpu.VMEM((s + CONV_HALO, CONV_DIM), F32)],
        compiler_params=_params(2),
        name="conv_prompt",
    )(glu, wdw, bdw)


def _conv_sample_kernel(st_ref, glu_ref, wdw_ref, bdw_ref, y_ref):
    acc = bdw_ref[...] + wdw_ref[CONV_STATE:CONV_STATE + 1, :] * glu_ref[...]
    for j in range(CONV_STATE):
        acc = acc + wdw_ref[j:j + 1, :] * st_ref[j]
    y_ref[...] = acc


def _conv_sample(state_t, glu, wdw, bdw):
    db = glu.shape[0]
    return pl.pallas_call(
        _conv_sample_kernel,
        grid=(1,),
        in_specs=[_full(state_t.shape), _full(glu.shape), _full(wdw.shape), _full(bdw.shape)],
        out_specs=_full((db, CONV_DIM)),
        out_shape=jax.ShapeDtypeStruct((db, CONV_DIM), F32),
        compiler_params=_params(1),
        name="conv_sample",
    )(state_t, glu, wdw, bdw)


def _rel_bucket_np(dist):
    max_exact = REL_BUCKETS // 2
    d = np.maximum(dist, 0)
    df = np.maximum(d, 1).astype(np.float32)
    large = max_exact + (np.log(df / max_exact) / math.log(REL_MAX_DIST / max_exact)
                         * (REL_BUCKETS - max_exact)).astype(np.int32)
    return np.where(d < max_exact, d, np.minimum(large, REL_BUCKETS - 1)).astype(np.int32)


def _topk_threshold(count_ge, shape, kk):
    def bit_body(bi, thr_u):
        cand_u = thr_u | jnp.left_shift(jnp.int32(1), 31 - bi)
        cnt = count_ge(cand_u ^ INT_MIN)
        return jnp.where(cnt >= kk, cand_u, thr_u)

    thr_u = lax.fori_loop(0, 32, bit_body, jnp.zeros(shape, I32))
    return thr_u ^ INT_MIN


def _attn_prompt_kernel(tq, n_toep, kk, cb_ref, qT_ref, qiT_ref, wiT_ref, kh_ref, vT_ref, ki_ref, toep_ref,
                        o_ref, keys_ref, m_ref, l_ref, acc_ref):
    tk = tq
    qt = pl.program_id(1)
    nk = qt + 1
    krow = lax.broadcasted_iota(I32, (tk, tq), 0)
    qcol = lax.broadcasted_iota(I32, (tk, tq), 1)

    def score_body(kt, carry):
        kic = ki_ref[0, kt]
        sc = jnp.zeros((tk, tq), F32)
        for h in range(IDX_HEADS):
            s = _dot(kic, qiT_ref[0, 0, h])
            sc = sc + jnp.maximum(s, 0.0) * (wiT_ref[0, h:h + 1, :] * IDX_HEADS ** -0.5)
        causal = (kt * tk + krow) <= (qt * tq + qcol)
        keys_ref[kt] = jnp.where(causal, _monotone_key(sc), INT_MIN)
        return carry

    lax.fori_loop(0, nk, score_body, 0)

    def count_ge(cand):
        def cnt_body(kt, c):
            hit = jnp.where(keys_ref[kt] >= cand, 1, 0)
            return c + jnp.sum(hit.reshape(tk // SUBLANES, SUBLANES, tq), axis=0)
        c = lax.fori_loop(0, nk, cnt_body, jnp.zeros((SUBLANES, tq), I32))
        return jnp.sum(c, axis=0, keepdims=True)

    thr = jnp.maximum(_topk_threshold(count_ge, (1, tq), kk), INT_MIN + 1)

    m_ref[...] = jnp.full(m_ref.shape, NEG, F32)
    l_ref[...] = jnp.zeros(l_ref.shape, F32)
    acc_ref[...] = jnp.zeros(acc_ref.shape, F32)

    def kv_step(kt, bias_of, thr_k):
        mask = keys_ref[kt] >= thr_k
        for h in range(N_HEADS):
            s = _dot(kh_ref[0, h, kt], qT_ref[0, 0, h]) + bias_of(h)
            m_old = m_ref[h]
            m_new = jnp.maximum(m_old, jnp.max(jnp.where(mask, s, NEG), axis=0, keepdims=True))
            p = jnp.where(mask, jnp.exp(s - m_new), 0.0)
            alpha = jnp.exp(m_old - m_new)
            l_ref[h] = alpha * l_ref[h] + jnp.sum(p, axis=0, keepdims=True)
            acc_ref[h] = alpha * acc_ref[h] + _dot(vT_ref[0, kt, h], p.astype(BF))
            m_ref[h] = m_new

    def far_body(kt, carry):
        kv_step(kt, lambda h: cb_ref[h], thr)
        return carry

    lax.fori_loop(0, jnp.maximum(nk - n_toep, 0), far_body, 0)
    for o in reversed(range(n_toep)):
        thr_o = jnp.where(qt >= o, thr, INT_MAX)
        kv_step(jnp.maximum(qt - o, 0), lambda h, o=o: toep_ref[o, h], thr_o)
    for h in range(N_HEADS):
        o_ref[0, h] = acc_ref[h] / l_ref[h]


def _attn_prompt(cb, qT, qiT, wiT, kh, vT, ki, toep, tq, kk):
    b, nq = qT.shape[:2]
    s = nq * tq
    n_toep = toep.shape[0]
    q_tile = lambda n: pl.BlockSpec((1, 1, N_HEADS, n, tq), lambda i, j: (i, j, 0, 0, 0))
    return pl.pallas_call(
        functools.partial(_attn_prompt_kernel, tq, n_toep, kk),
        grid=(b, nq),
        in_specs=[pl.BlockSpec(memory_space=pltpu.SMEM),
                  q_tile(HEAD_DIM), q_tile(IDX_DIM),
                  pl.BlockSpec((1, IDX_HEADS, tq), lambda i, j: (i, 0, j)),
                  pl.BlockSpec((1, N_HEADS, nq, tq, HEAD_DIM), lambda i, j: (i, 0, 0, 0, 0)),
                  pl.BlockSpec((1, nq, N_HEADS, HEAD_DIM, tq), lambda i, j: (i, 0, 0, 0, 0)),
                  pl.BlockSpec((1, nq, tq, IDX_DIM), lambda i, j: (i, 0, 0, 0)),
                  _full(toep.shape)],
        out_specs=pl.BlockSpec((1, N_HEADS, HEAD_DIM, tq), lambda i, j: (i, 0, 0, j)),
        out_shape=jax.ShapeDtypeStruct((b, N_HEADS, HEAD_DIM, s), F32),
        scratch_shapes=[pltpu.VMEM((nq, tq, tq), I32),
                        pltpu.VMEM((N_HEADS, 1, tq), F32), pltpu.VMEM((N_HEADS, 1, tq), F32),
                        pltpu.VMEM((N_HEADS, HEAD_DIM, tq), F32)],
        compiler_params=_params(2),
        name="attn_prompt",
    )(cb, qT, qiT, wiT, kh, vT, ki, toep)


PAGES_PER_STEP = 32


def _page_specs(block, n_pages, per_step=PAGES_PER_STEP):
    def spec(r):
        return pl.BlockSpec(block, lambda b, g, pt: (0, pt[b * n_pages + g * per_step + r]) + (0,) * (len(block) - 2))
    return [spec(r) for r in range(per_step)]


def _samp_scores_kernel(pt_ref, qi_ref, wi_ref, *refs):
    pages, o_ref = refs[:-1], refs[-1]
    qi = qi_ref[0]
    w = wi_ref[0] * IDX_HEADS ** -0.5
    rows = []
    for page in pages:
        s = _dot(qi, page[0, 0].astype(BF))
        rows.append(jnp.sum(jnp.maximum(s, 0.0) * w, axis=0, keepdims=True))
    o_ref[0] = jnp.concatenate(rows, axis=0)


def _samp_scores(page_table_flat, qi, wi, cache_idx_t, n_pages):
    db = qi.shape[0]
    grid_spec = pltpu.PrefetchScalarGridSpec(
        num_scalar_prefetch=1,
        grid=(db, 1),
        in_specs=[pl.BlockSpec((1, IDX_HEADS, IDX_DIM), lambda b, g, pt: (b, 0, 0)),
                  pl.BlockSpec((1, IDX_HEADS, 1), lambda b, g, pt: (b, 0, 0))]
                 + _page_specs((1, 1, IDX_DIM, PAGE_SIZE), n_pages, n_pages),
        out_specs=pl.BlockSpec((1, n_pages, PAGE_SIZE), lambda b, g, pt: (b, 0, 0)),
    )
    return pl.pallas_call(
        _samp_scores_kernel,
        grid_spec=grid_spec,
        out_shape=jax.ShapeDtypeStruct((db, n_pages, PAGE_SIZE), F32),
        compiler_params=_params(2),
        name="samp_scores",
    )(page_table_flat, qi, wi, *([cache_idx_t] * n_pages))


SEQ_PER_STEP = 8


def _total(x):
    return jnp.sum(jnp.sum(x, axis=0, keepdims=True), axis=1, keepdims=True)


def _ind(m):
    return jnp.where(m, 1.0, 0.0)


def _samp_select_kernel(kk, sc_ref, qi_ref, ki8_ref, kw_ref, tri_ref, lt_ref, sel_ref, nsel_ref,
                        keys_ref, thr_ref, knew_ref):
    key_new = []
    for r in range(SEQ_PER_STEP):
        prod = qi_ref[r].astype(F32) * ki8_ref[r].astype(F32)
        kwv = kw_ref[r]
        s_new = jnp.zeros((1, 1), F32)
        for h in range(IDX_HEADS):
            s = jnp.sum(prod[:, h * IDX_DIM:(h + 1) * IDX_DIM], axis=1, keepdims=True)
            s_new = s_new + jnp.maximum(s, 0.0) * (kwv[:, IDX_DIM + h:IDX_DIM + h + 1] * IDX_HEADS ** -0.5)
        key_new.append(_monotone_key(s_new))
        keys_ref[r] = _monotone_key(sc_ref[r])

    def bit_body(bi, thr_us):
        bit = jnp.left_shift(jnp.int32(1), 31 - bi)
        out = []
        for r in range(SEQ_PER_STEP):
            cand_u = thr_us[r] | bit
            cand = cand_u ^ INT_MIN
            cnt = _total(jnp.where(keys_ref[r] >= cand, 1, 0)) + jnp.where(key_new[r] >= cand, 1, 0)
            out.append(jnp.where(cnt >= kk, cand_u, thr_us[r]))
        return tuple(out)

    thr_us = lax.fori_loop(0, 32, bit_body, tuple(jnp.zeros((1, 1), I32) for _ in range(SEQ_PER_STEP)))
    for r in range(SEQ_PER_STEP):
        thr_ref[r] = jnp.broadcast_to(thr_us[r] ^ INT_MIN, (1, LANES))
        knew_ref[r] = jnp.broadcast_to(key_new[r], (1, LANES))

    def prefix(x):
        incl = _dot(x.astype(BF), tri_ref[...])
        page_tot = jnp.broadcast_to(incl[:, PAGE_SIZE - 1:PAGE_SIZE], incl.shape)
        return incl - x + _dot(lt_ref[...], page_tot.astype(BF))

    def seq_body(r, carry):
        keys = keys_ref[r]
        thr = thr_ref[r][:, :1]
        kn = knew_ref[r][:, :1]
        gt = keys > thr
        eq = keys == thr
        eq_f = _ind(eq)
        need = kk - (_total(_ind(gt)) + _ind(kn > thr))
        sel = jnp.where(gt, 1.0, jnp.where(eq, _ind(prefix(eq_f) < need), 0.0))
        new_sel = jnp.where(kn > thr, 1.0, jnp.where(kn == thr, _ind(_total(eq_f) < need), 0.0))
        sel_ref[r] = sel
        nsel_ref[r] = jnp.broadcast_to(new_sel, (1, LANES))
        return carry

    lax.fori_loop(0, SEQ_PER_STEP, seq_body, 0)


def _samp_select(scores, qib, ki8, kw, kk):
    db, n_pages, _ = scores.shape
    spb = SEQ_PER_STEP
    assert db % spb == 0
    pos = np.arange(PAGE_SIZE)
    tri = jnp.asarray(pos[:, None] <= pos[None, :], BF)
    pg = np.arange(n_pages)
    lt = jnp.asarray(pg[None, :] < pg[:, None], BF)
    per_seq = lambda n: pl.BlockSpec((spb, 1, n), lambda i: (i, 0, 0))
    return pl.pallas_call(
        functools.partial(_samp_select_kernel, kk),
        grid=(db // spb,),
        in_specs=[pl.BlockSpec((spb, n_pages, PAGE_SIZE), lambda i: (i, 0, 0)),
                  per_seq(qib.shape[2]), per_seq(ki8.shape[2]), per_seq(kw.shape[2]),
                  _full(tri.shape), _full(lt.shape)],
        out_specs=[pl.BlockSpec((spb, n_pages, PAGE_SIZE), lambda i: (i, 0, 0)), per_seq(LANES)],
        out_shape=[jax.ShapeDtypeStruct((db, n_pages, PAGE_SIZE), F32),
                   jax.ShapeDtypeStruct((db, 1, LANES), F32)],
        scratch_shapes=[pltpu.VMEM((spb, n_pages, PAGE_SIZE), I32),
                        pltpu.VMEM((spb, 1, LANES), I32), pltpu.VMEM((spb, 1, LANES), I32)],
        compiler_params=_params(1),
        name="samp_select",
    )(scores, qib, ki8, kw, tri, lt)


def _samp_attn_kernel(pt_ref, q_ref, qt_ref, sel_ref, bias_ref, bnew_ref, kn_ref, vnt_ref, *refs):
    k_pages = refs[:PAGES_PER_STEP]
    v_pages = refs[PAGES_PER_STEP:2 * PAGES_PER_STEP]
    o_ref, m_ref, l_ref, acc_ref, qb_ref = refs[2 * PAGES_PER_STEP:]
    g = pl.program_id(1)

    @pl.when(g == 0)
    def _():
        m_ref[...] = jnp.full(m_ref.shape, NEG, F32)
        l_ref[...] = jnp.zeros(l_ref.shape, F32)
        acc_ref[...] = jnp.zeros(acc_ref.shape, F32)
        for h in range(N_HEADS):
            qb_ref[h] = jnp.broadcast_to(qt_ref[0][:, h:h + 1], (HEAD_DIM, PAGE_SIZE))

    logits, masks = [], []
    for r in range(PAGES_PER_STEP):
        rows = [jnp.sum(k_pages[r][0, 0, h] * qb_ref[h], axis=0, keepdims=True) for h in range(N_HEADS)]
        mask = sel_ref[0, r:r + 1, :] > 0.5
        masks.append(mask)
        logits.append(jnp.where(mask, jnp.concatenate(rows, axis=0) + bias_ref[0, r], NEG))
    m_old = m_ref[...]
    m_new = m_old
    for s in logits:
        m_new = jnp.maximum(m_new, jnp.max(s, axis=1, keepdims=True))
    alpha = jnp.exp(m_old - m_new)
    probs = [jnp.where(mask, jnp.exp(s - m_new), 0.0) for s, mask in zip(logits, masks)]
    l = alpha * l_ref[...]
    for p in probs:
        l = l + jnp.sum(p, axis=1, keepdims=True)
    l_ref[...] = l
    m_ref[...] = m_new
    for h in range(N_HEADS):
        acc = acc_ref[h] * alpha[h:h + 1, :]
        for r in range(PAGES_PER_STEP):
            acc = acc + v_pages[r][0, 0, h] * probs[r][h:h + 1, :]
        acc_ref[h] = acc

    @pl.when(g == pl.num_programs(1) - 1)
    def _():
        s_new = jnp.sum(kn_ref[0] * q_ref[0], axis=1, keepdims=True) + bnew_ref[0]
        m_fin = jnp.maximum(m_new, s_new)
        a_fin = jnp.exp(m_new - m_fin)
        p_new = jnp.exp(s_new - m_fin)
        l_fin = a_fin * l + p_new
        cols = []
        for h in range(N_HEADS):
            past = jnp.sum(acc_ref[h], axis=1, keepdims=True) * a_fin[h:h + 1, :]
            cols.append((past + vnt_ref[0][:, h:h + 1] * p_new[h:h + 1, :]) / l_fin[h:h + 1, :])
        o_ref[0] = jnp.concatenate(cols, axis=1)


def _samp_attn(page_table_flat, q, qt, sel, bias_t, bias_new, kn, vnt, cache_k_t, cache_v_t, n_pages):
    db = q.shape[0]
    ng = n_pages // PAGES_PER_STEP
    per_seq = lambda shape: pl.BlockSpec((1,) + shape, lambda b, g, pt: (b,) + (0,) * len(shape))
    page_block = (1, 1, N_HEADS, HEAD_DIM, PAGE_SIZE)
    grid_spec = pltpu.PrefetchScalarGridSpec(
        num_scalar_prefetch=1,
        grid=(db, ng),
        in_specs=[per_seq((N_HEADS, HEAD_DIM)), per_seq((HEAD_DIM, N_HEADS)),
                  pl.BlockSpec((1, PAGES_PER_STEP, PAGE_SIZE), lambda b, g, pt: (b, g, 0)),
                  pl.BlockSpec((1, PAGES_PER_STEP, N_HEADS, PAGE_SIZE), lambda b, g, pt: (g, 0, 0, 0)),
                  per_seq((N_HEADS, 1)), per_seq((N_HEADS, HEAD_DIM)), per_seq((HEAD_DIM, N_HEADS))]
                 + _page_specs(page_block, n_pages) + _page_specs(page_block, n_pages),
        out_specs=per_seq((HEAD_DIM, N_HEADS)),
        scratch_shapes=[pltpu.VMEM((N_HEADS, 1), F32), pltpu.VMEM((N_HEADS, 1), F32),
                        pltpu.VMEM((N_HEADS, HEAD_DIM, PAGE_SIZE), F32),
                        pltpu.VMEM((N_HEADS, HEAD_DIM, PAGE_SIZE), F32)],
    )
    return pl.pallas_call(
        _samp_attn_kernel,
        grid_spec=grid_spec,
        out_shape=jax.ShapeDtypeStruct((db, HEAD_DIM, N_HEADS), F32),
        compiler_params=_params(2),
        name="samp_attn",
    )(page_table_flat, q, qt, sel, bias_t, bias_new, kn, vnt,
      *([cache_k_t] * PAGES_PER_STEP), *([cache_v_t] * PAGES_PER_STEP))


def _merge_kernel(per_row, x_ref, yc_ref, at_ref, sgc_ref, sga_ref, mod_ref, lng_ref, lnb_ref,
                  wco_ref, wao_ref, wout_ref, g2_ref, wpq_ref, kbd_ref,
                  xm_ref, h2t_ref, st_ref):
    yc = yc_ref[...]
    mu = jnp.mean(yc, axis=-1, keepdims=True)
    yd = yc - mu
    ln = yd * lax.rsqrt(jnp.mean(yd * yd, axis=-1, keepdims=True) + EPS) * lng_ref[...] + lnb_ref[...]
    act = ln * jax.nn.sigmoid(ln)
    conv_out = _dot(act.astype(BF), wco_ref[...])
    attn_out = _dot(at_ref[0].T.astype(BF), wao_ref[...])
    merged = sgc_ref[...] * conv_out + sga_ref[...] * attn_out
    xm = x_ref[...] + _mod(mod_ref, 2, per_row) * _dot(merged.astype(BF), wout_ref[...])
    xm_ref[...] = xm
    rs = lax.rsqrt(jnp.mean(xm * xm, axis=-1, keepdims=True) + EPS)
    h2 = (xm * rs) * g2_ref[...] * (1.0 + _mod(mod_ref, 4, per_row)) + _mod(mod_ref, 3, per_row)
    h2t_ref[...] = pltpu.bitcast(h2.T.astype(BF), I32)
    pq = _dot(h2.astype(BF), wpq_ref[...])
    st_ref[...] = _dot_nt(kbd_ref[...], pq.astype(BF))


def _merge(x, yc, at, sgc, sga, mod, per_row, rows_per_seq, tm, wts):
    t = x.shape[0]
    row = lambda n: pl.BlockSpec((tm, n), lambda i: (i, 0))
    col = lambda n: pl.BlockSpec((n, tm), lambda i: (0, i))
    n_scores = PEER_HEADS * 2 * PEER_KEYS
    return pl.pallas_call(
        functools.partial(_merge_kernel, per_row),
        grid=(t // tm,),
        in_specs=[row(D_MODEL), row(CONV_DIM),
                  pl.BlockSpec((1, ATTN_DIM, tm), lambda i: (i // (at.shape[2] // tm), 0, i % (at.shape[2] // tm))),
                  row(D_MODEL), row(D_MODEL),
                  _mod_spec(per_row, tm, rows_per_seq)] + [_full(w.shape) for w in wts],
        out_specs=[row(D_MODEL), col(D_MODEL // 2), col(n_scores)],
        out_shape=[jax.ShapeDtypeStruct((t, D_MODEL), F32),
                   jax.ShapeDtypeStruct((D_MODEL // 2, t), I32),
                   jax.ShapeDtypeStruct((n_scores, t), F32)],
        compiler_params=_params(1),
        name="merge",
    )(x, yc, at, sgc, sga, mod, *wts)


N_EXTRACT = PEER_TOPK + 1


def _top_extract(cur, n):
    outs = []
    for _ in range(n):
        mx = jnp.max(cur, axis=0, keepdims=True)
        outs.append(mx)
        cur = jnp.where(cur >= mx, -jnp.inf, cur)
    return outs


RANK_NONE = float(PEER_KEYS - 1)


def _peer_select_kernel(lb, st_ref, r1_ref, e1_ref, n_ref, al_ref):
    tl = st_ref.shape[1]

    def head_body(h, carry):
        base = pl.multiple_of(h * 2 * PEER_KEYS, 2 * PEER_KEYS)
        s0 = st_ref[pl.ds(base, PEER_KEYS), :]
        s1 = st_ref[pl.ds(base + PEER_KEYS, PEER_KEYS), :]
        a0 = _top_extract(s0, N_EXTRACT)
        a1 = []
        cur = s1
        rank1 = jnp.full(s1.shape, RANK_NONE, F32)
        for k in range(N_EXTRACT):
            mx = jnp.max(cur, axis=0, keepdims=True)
            a1.append(mx)
            hit = cur >= mx
            if k < PEER_TOPK:
                rank1 = jnp.where(hit, float(k), rank1)
            cur = jnp.where(hit, -jnp.inf, cur)
        pad = [jnp.full((1, tl), -jnp.inf, F32)] * (3 * SUBLANES - N_EXTRACT)
        a1_all = jnp.concatenate(a1 + pad, axis=0)
        cands = [a0[0] + a1_all] + [a0[k] + a1_all[:SUBLANES] for k in range(1, N_EXTRACT)]
        top = _top_extract(jnp.concatenate(cands, axis=0), N_EXTRACT)
        thr = 0.5 * (top[PEER_TOPK - 1] + top[PEER_TOPK])
        z = jnp.zeros_like(thr)
        for k in range(PEER_TOPK):
            z = z + jnp.exp(top[k] - top[0])
        tau = thr - s0
        cnt = jnp.zeros(s0.shape, F32)
        for l in range(PEER_TOPK):
            cnt = cnt + jnp.where(a1[l] >= tau, 1.0, 0.0)
        e1 = jnp.exp(s1 - a1[0])
        al = jnp.exp(s0 - a0[0]) / z
        for tb in range(tl // lb):
            ts = slice(tb * lb, (tb + 1) * lb)
            hi_bits = lambda v: lax.bitcast_convert_type(v[:, ts].astype(BF).astype(F32), I32)
            both = lambda b: b | lax.shift_right_logical(b, 16)
            half = PEER_KEYS // 2
            pair = lambda b: lax.shift_right_logical(b[:half], 16) | b[half:]
            r1_ref[h, tb] = both(hi_bits(rank1))
            e1_ref[h, tb] = both(hi_bits(e1))
            n_ref[h, tb] = pair(hi_bits(cnt))
            al_ref[h, tb] = pair(hi_bits(al))
        return carry

    lax.fori_loop(0, PEER_HEADS, head_body, 0)


def _peer_select(st, tl):
    t = st.shape[1]
    lb = min(LANES, t)
    spec = lambda rows: pl.BlockSpec((PEER_HEADS, tl // lb, rows, lb), lambda i: (0, i, 0, 0))
    shp = lambda rows: jax.ShapeDtypeStruct((PEER_HEADS, t // lb, rows, lb), I32)
    per_j, per_i = PEER_KEYS, PEER_KEYS // 2
    return pl.pallas_call(
        functools.partial(_peer_select_kernel, lb),
        grid=(t // tl,),
        in_specs=[pl.BlockSpec((st.shape[0], tl), lambda i: (0, i))],
        out_specs=[spec(per_j), spec(per_j), spec(per_i), spec(per_i)],
        out_shape=[shp(per_j), shp(per_j), shp(per_i), shp(per_i)],
        compiler_params=_params(1),
        name="peer_select",
    )(st)


PEER_PAIRS_PER_STEP = SUBLANES


def _peer_main_kernel(per_row, h2t_ref, ulo_ref, uhi_ref, vtlo_ref, vthi_ref, r1_ref, e1_ref, n_ref, al_ref,
                      xm_ref, mod_ref, o_ref, acc_ref, a_ref, g_ref):
    j = pl.program_id(1)
    tm = h2t_ref.shape[1]
    n_tb, _, lb = r1_ref.shape[1:]

    @pl.when(j == 0)
    def _():
        acc_ref[...] = jnp.zeros(acc_ref.shape, F32)

    i0 = pl.multiple_of(j * PEER_PAIRS_PER_STEP, PEER_PAIRS_PER_STEP)
    h2t = pltpu.bitcast(h2t_ref[...], BF)
    unit = 2 * PEER_KEYS
    n_units = PEER_PAIRS_PER_STEP // 2
    jn = PEER_KEYS
    u_refs = (ulo_ref, uhi_ref)
    vt_refs = (vtlo_ref, vthi_ref)

    def unpacked(ref, lo, n):
        return pltpu.bitcast(ref[lo:lo + n], BF)

    def activations(un):
        for c in range(2):
            a_ref[un % 2, c] = _dot(unpacked(u_refs[c], un * unit // 2, unit // 2), h2t)

    activations(0)
    for un in range(n_units):
        rows = slice(un * unit, (un + 1) * unit)
        if un + 1 < n_units:
            activations(un + 1)
        for tb in range(n_tb):
            ts = slice(tb * lb, (tb + 1) * lb)
            for jb in range(PEER_KEYS // jn):
                js = slice(jb * jn, (jb + 1) * jn)
                w = [jnp.zeros((2 * jn, lb), BF), jnp.zeros((2 * jn, lb), BF)]
                for h in range(PEER_HEADS):
                    cnt = n_ref[h, tb, pl.ds(i0, PEER_PAIRS_PER_STEP), :]
                    al = al_ref[h, tb, pl.ds(i0, PEER_PAIRS_PER_STEP), :]
                    r1 = pltpu.bitcast(r1_ref[h, tb, js, :], BF)
                    e1 = pltpu.bitcast(e1_ref[h, tb, js, :], BF)
                    for q in range(2):
                        pp = 2 * un + q
                        cnt_b = pltpu.bitcast(jnp.broadcast_to(cnt[pp:pp + 1], (jn, lb)), BF)
                        al_b = pltpu.bitcast(jnp.broadcast_to(al[pp:pp + 1], (jn, lb)), BF)
                        w[q] = w[q] + jnp.where(r1 < cnt_b, e1 * al_b, 0)
                for q in range(2):
                    wi = pltpu.bitcast(w[q], I32)
                    halves = (lax.bitcast_convert_type(lax.shift_left(wi, 16), F32),
                              lax.bitcast_convert_type(wi & jnp.int32(-65536), F32))
                    for c in range(2):
                        lo = q * PEER_KEYS + jb * jn
                        a = a_ref[un % 2, c, lo:lo + jn, ts]
                        gelu = 0.5 * a * (1.0 + lax.erf(a * (2.0 ** -0.5)))
                        g_ref[c, un * unit + lo:un * unit + lo + jn, ts] = (halves[c] * gelu).astype(BF)
        acc_ref[...] += (_dot(pltpu.bitcast(vtlo_ref[:, rows], BF), g_ref[0, rows, :])
                         + _dot(pltpu.bitcast(vthi_ref[:, rows], BF), g_ref[1, rows, :]))

    @pl.when(j == pl.num_programs(1) - 1)
    def _():
        o_ref[...] = xm_ref[...] + _mod(mod_ref, 5, per_row) * acc_ref[...].T


def _peer_main(h2t, u_pk, vt_pk, r1, e1, cnt, al, xm, mod, per_row, rows_per_seq, tm):
    t = xm.shape[0]
    ec = PEER_PAIRS_PER_STEP * PEER_KEYS
    n_steps = u_pk.shape[0] * 2 // (2 * ec)
    lb = r1.shape[3]
    sel = lambda rows: pl.BlockSpec((PEER_HEADS, tm // lb, rows, lb), lambda i, j: (0, i, 0, 0))
    if per_row:
        mod_spec = pl.BlockSpec((6, tm, D_MODEL), lambda i, j: (0, i, 0))
    else:
        tiles_per_seq = rows_per_seq // tm
        mod_spec = pl.BlockSpec((1, 6, D_MODEL), lambda i, j: (i // tiles_per_seq, 0, 0))
    return pl.pallas_call(
        functools.partial(_peer_main_kernel, per_row),
        grid=(t // tm, n_steps),
        in_specs=[pl.BlockSpec((D_MODEL // 2, tm), lambda i, j: (0, i)),
                  pl.BlockSpec((ec // 2, D_MODEL), lambda i, j: (j, 0)),
                  pl.BlockSpec((ec // 2, D_MODEL), lambda i, j: (j + n_steps, 0)),
                  pl.BlockSpec((D_MODEL // 2, ec), lambda i, j: (0, j)),
                  pl.BlockSpec((D_MODEL // 2, ec), lambda i, j: (0, j + n_steps)),
                  sel(PEER_KEYS), sel(PEER_KEYS), sel(PEER_KEYS // 2), sel(PEER_KEYS // 2),
                  pl.BlockSpec((tm, D_MODEL), lambda i, j: (i, 0)),
                  mod_spec],
        out_specs=pl.BlockSpec((tm, D_MODEL), lambda i, j: (i, 0)),
        out_shape=jax.ShapeDtypeStruct((t, D_MODEL), F32),
        scratch_shapes=[pltpu.VMEM((D_MODEL, tm), F32), pltpu.VMEM((2, 2, 2 * PEER_KEYS, tm), F32),
                        pltpu.VMEM((2, ec, tm), BF)],
        compiler_params=_params(2),
        name="peer_main",
    )(h2t, u_pk, u_pk, vt_pk, vt_pk, r1, e1, cnt, al, xm, mod)


PACK_ROWS = 512


def _peer_pack_kernel(u_ref, v_ref, u_pk_ref, vt_pk_ref):
    u_pk_ref[...] = pltpu.bitcast(u_ref[...].astype(BF), I32)
    vt_pk_ref[...] = pltpu.bitcast(v_ref[...].T.astype(BF), I32)


def _peer_pack(u, v):
    n_e = u.shape[0]
    return pl.pallas_call(
        _peer_pack_kernel,
        grid=(n_e // PACK_ROWS,),
        in_specs=[pl.BlockSpec((PACK_ROWS, D_MODEL), lambda i: (i, 0))] * 2,
        out_specs=[pl.BlockSpec((PACK_ROWS // 2, D_MODEL), lambda i: (i, 0)),
                   pl.BlockSpec((D_MODEL // 2, PACK_ROWS), lambda i: (0, i))],
        out_shape=[jax.ShapeDtypeStruct((n_e // 2, D_MODEL), I32),
                   jax.ShapeDtypeStruct((D_MODEL // 2, n_e), I32)],
        compiler_params=_params(1),
        name="peer_pack",
    )(u, v)


def _tile(n, pref):
    return pref if n % pref == 0 else n


def kernel(x_prompt, x_sample, cache_k, cache_v, cache_idx_k, state_conv, page_table, c_prompt, c_sample,
           w_ada, b_ada, norm1_g, w_in, q_norm_g, k_norm_g, rel_bias, w_dw, b_dw, conv_ln_g, conv_ln_b,
           w_conv_o, w_attn_o, w_out, norm2_g, w_peer_q, peer_keys, peer_u, peer_v):
    b, s, _ = x_prompt.shape
    db, ds, _ = x_sample.shape
    n_pages = page_table.shape[1]
    past = n_pages * PAGE_SIZE
    assert w_ada.shape[0] == 1, "one layer"
    assert ds == 1, "one new token per sample sequence"
    assert n_pages % PAGES_PER_STEP == 0
    tp = b * s

    w = w_in[0]
    pts = np.cumsum([0, 2 * CONV_DIM, ATTN_DIM, ATTN_DIM, ATTN_DIM, IDX_HEADS * IDX_DIM, IDX_DIM, IDX_HEADS,
                     D_MODEL, D_MODEL])
    seg = lambda i: w[:, pts[i]:pts[i + 1]].astype(BF)
    w_kw = jnp.pad(w[:, pts[5]:pts[7]], ((0, 0), (0, LANES - IDX_DIM - IDX_HEADS))).astype(BF)
    head_of = np.arange(ATTN_DIM) // HEAD_DIM
    bd = jnp.asarray(head_of[:, None] == head_of[None, :], BF)
    row = lambda v: v.reshape(1, -1)
    in_wts = [row(norm1_g[0]), seg(0), seg(1), seg(2), seg(3), seg(4), w_kw, seg(7), seg(8),
              row(jnp.tile(q_norm_g[0], N_HEADS)), row(jnp.tile(k_norm_g[0], N_HEADS)), bd]
    n_half = PEER_HEADS * 2
    kb16 = peer_keys[0].reshape(n_half, PEER_KEYS, PEER_HALF)
    kbd_t = (jnp.eye(n_half, dtype=F32)[:, None, :, None] * kb16[:, :, None, :]).reshape(
        n_half * PEER_KEYS, n_half * PEER_HALF).astype(BF)
    merge_wts = [row(conv_ln_g[0]), row(conv_ln_b[0]), w_conv_o[0].astype(BF), w_attn_o[0].astype(BF),
                 w_out[0].astype(BF), row(norm2_g[0]), w_peer_q[0].astype(BF), kbd_t]
    u_pk, vt_pk = _peer_pack(peer_u[0], peer_v[0])

    mod = _ada(jnp.concatenate([c_prompt, c_sample], axis=0), w_ada[0].astype(BF), row(b_ada[0]))
    mod_p = mod[:b].reshape(b, 6, D_MODEL)
    mod_s = mod[b:].reshape(db, 6, D_MODEL).transpose(1, 0, 2)

    tq = _tile(s, 256)
    nq = s // tq
    glu_p, sgc_p, sga_p, kT_p, vT32_p, kiT_p, qT_p, qiT_p, vT_p, kh_p, ki_p, wiT_p = _inproj(
        x_prompt.reshape(tp, D_MODEL), mod_p, False, s, tq, in_wts)
    glu_s, sgc_s, sga_s, k_s, v_s, kw_s, qb_s, qib_s = _inproj(
        x_sample.reshape(db, D_MODEL), mod_s, True, 1, db, in_wts)

    yc_p = _conv_prompt(glu_p.reshape(b, s, CONV_DIM), w_dw[0], row(b_dw[0]), _tile(s, 256)).reshape(tp, CONV_DIM)
    yc_s = _conv_sample(state_conv[0].transpose(1, 0, 2), glu_s, w_dw[0], row(b_dw[0]))

    bucket =_rel_bucket_np(np.arange(max(s, past + 1)))
    n_toep = nq
    while n_toep > 0 and len(set(bucket[max((n_toep - 1) * tq - (tq - 1), 0):n_toep * tq])) == 1 \
            and bucket[(n_toep - 1) * tq] == bucket[s - 1]:
        n_toep -= 1
    n_toep = max(n_toep, 1)

    def bias_of_bucket(bkt):
        onehot = jnp.asarray(np.eye(REL_BUCKETS, dtype=np.float32)[bkt.reshape(-1)])
        return jnp.dot(onehot, rel_bias, precision=lax.Precision.HIGHEST).reshape(bkt.shape + (N_HEADS,))

    width = 2 * tq + 1
    dist = (np.arange(n_toep)[:, None] - 1) * tq + np.arange(width)[None, :]
    table = bias_of_bucket(_rel_bucket_np(dist)).transpose(0, 2, 1)
    skew = jnp.tile(table, (1, 1, tq))[:, :, :tq * (width - 1)].reshape(n_toep, N_HEADS, tq, width - 1)
    toep = skew[:, :, :, tq:]
    cb = rel_bias[int(bucket[s - 1])]

    kk_p = min(IDX_TOPK_MAX, s // 4)
    split_heads = lambda a: a.reshape(b, nq, N_HEADS, -1, tq)
    at_p = _attn_prompt(cb, split_heads(qT_p), split_heads(qiT_p), wiT_p, kh_p, split_heads(vT_p), ki_p,
                        toep, tq, kk_p)
    at_p = at_p.reshape(b, ATTN_DIM, s)

    kk_s = min(IDX_TOPK_MAX, (past + ds) // 4)
    pt_flat = page_table.reshape(-1)
    sc = _samp_scores(pt_flat, qib_s.reshape(db, IDX_HEADS, IDX_DIM),
                      kw_s[:, IDX_DIM:IDX_DIM + IDX_HEADS].reshape(db, IDX_HEADS, 1),
                      cache_idx_k.transpose(0, 1, 3, 2), n_pages)
    ki8 = jnp.tile(kw_s[:, :IDX_DIM].astype(BF), (1, IDX_HEADS))
    sel_s, new_sel = _samp_select(sc.reshape(db, n_pages, PAGE_SIZE), qib_s[:, None, :], ki8[:, None, :],
                                  kw_s[:, None, :], kk_s)
    pos = np.arange(past).reshape(n_pages // PAGES_PER_STEP, PAGES_PER_STEP, PAGE_SIZE)
    bias_t = bias_of_bucket(_rel_bucket_np(past - pos)).transpose(0, 1, 3, 2)
    bias_new = jnp.where(new_sel[:, 0, :1] > 0.0, rel_bias[int(_rel_bucket_np(np.zeros((), np.int64)))][None], NEG)
    heads = lambda a: a.reshape(db, N_HEADS, HEAD_DIM)
    q_s = heads(qb_s.astype(F32))
    at_s = _samp_attn(pt_flat, q_s, q_s.transpose(0, 2, 1), sel_s, bias_t, bias_new.reshape(db, N_HEADS, 1),
                      heads(k_s), heads(v_s).transpose(0, 2, 1),
                      cache_k.transpose(0, 1, 3, 4, 2), cache_v.transpose(0, 1, 3, 4, 2), n_pages)
    at_s = at_s.transpose(2, 1, 0).reshape(1, ATTN_DIM, db)

    def tail(x, yc, at, sgc, sga, mod_g, per_row, rows_per_seq, tm_merge, tl_sel, tm_peer):
        xm, h2t, st = _merge(x, yc, at, sgc, sga, mod_g, per_row, rows_per_seq, tm_merge, merge_wts)
        s1, e1, tau, al = _peer_select(st, tl_sel)
        return _peer_main(h2t, u_pk, vt_pk, s1, e1, tau, al, xm, mod_g, per_row, rows_per_seq, tm_peer)

    y_p = tail(x_prompt.reshape(tp, D_MODEL), yc_p, at_p, sgc_p, sga_p, mod_p, False, s,
               _tile(s, 256), _tile(tp, 256), _tile(s, 512))
    y_s = tail(x_sample.reshape(db, D_MODEL), yc_s, at_s, sgc_s, sga_s, mod_s, True, 1, db, db, db)

    glu_p3 = glu_p.reshape(b, s, CONV_DIM)
    conv_s = jnp.concatenate([state_conv[0], glu_s[:, None, :]], axis=1)[:, -CONV_STATE:]
    seq_major = lambda a: a.reshape(b, N_HEADS, HEAD_DIM, s).transpose(0, 3, 1, 2)[None]
    return (y_p.reshape(b, s, D_MODEL), y_s.reshape(db, ds, D_MODEL),
            seq_major(kT_p), seq_major(vT32_p),
            kiT_p.transpose(0, 2, 1)[None], glu_p3[None, :, -CONV_STATE:],
            k_s.reshape(1, db, ds, N_HEADS, HEAD_DIM), v_s.reshape(1, db, ds, N_HEADS, HEAD_DIM),
            kw_s[:, :IDX_DIM].reshape(1, db, ds, IDX_DIM), conv_s[None])
```
